```python
import math
import jax, jax.numpy as jnp
from jax import lax
import numpy as np

D_MODEL = 1024
BATCH = 16
SEQ = 4096
DEPTH = 1
DEC_BATCH = 32
DEC_SEQ = 16
PAST_LEN = 1024

CHUNK = 64
N_HEADS = 4
QK_DIM = 64
V_DIM = 128
ATTN_WIDTH = N_HEADS * V_DIM
POOL_WIDTH = D_MODEL - ATTN_WIDTH
POOL_WINDOWS = (2, 4, 8, 16)
N_POOL_GROUPS = len(POOL_WINDOWS)
POOL_GROUP = POOL_WIDTH // N_POOL_GROUPS
POOL_STATE = max(POOL_WINDOWS) - 1
QK_WIDTH = N_HEADS * 2 * QK_DIM
IN_WIDTH = 2 * QK_WIDTH + ATTN_WIDTH + POOL_WIDTH
N_EXPERTS = 32
TOP_K = 4
D_FF = D_MODEL
SWIGLU_LIMIT = 7.0
SWIGLU_ALPHA = 1.702
MOE_BLOCK = 128
Q_BLOCK = 128
EPS = 1e-6

kernel_name = "hymba_diffattn_pool_moe_stream_step"


def rmsnorm(x, g):
    xf = x.astype(jnp.float32)
    y = xf * lax.rsqrt(jnp.mean(xf * xf, axis=-1, keepdims=True) + EPS)
    return (y * g.astype(jnp.float32)).astype(x.dtype)


def project(hn, w_in, q_norm, k_norm):
    b, q_len, _ = hn.shape
    z = hn @ w_in
    q = z[..., :QK_WIDTH].reshape(b, q_len, N_HEADS, 2, QK_DIM)
    k = z[..., QK_WIDTH:2 * QK_WIDTH].reshape(b, q_len, N_HEADS, 2, QK_DIM)
    v = z[..., 2 * QK_WIDTH:2 * QK_WIDTH + ATTN_WIDTH].reshape(b, q_len, N_HEADS, V_DIM)
    u = z[..., 2 * QK_WIDTH + ATTN_WIDTH:]
    return rmsnorm(q, q_norm), rmsnorm(k, k_norm), v, u


def diff_attend(q, k, v, q_pos, k_pos, lam):
    s = jnp.einsum('bqhcd,bkhcd->bhcqk', q.astype(jnp.float32), k.astype(jnp.float32)) * (QK_DIM ** -0.5)
    visible = (k_pos[None, :] // CHUNK) <= (q_pos[:, None] // CHUNK)
    s = jnp.where(visible, s, -jnp.inf)
    p = jax.nn.softmax(s, axis=-1)
    pd = p[:, :, 0] - lam * p[:, :, 1]
    return jnp.einsum('bhqk,bkhd->bqhd', pd, v.astype(jnp.float32))


def prompt_attention(q, k, v, lam):
    b, s = q.shape[:2]
    nb = s // Q_BLOCK
    qb = q.reshape(b, nb, Q_BLOCK, N_HEADS, 2, QK_DIM).transpose(1, 0, 2, 3, 4, 5)
    qpos = jnp.arange(s).reshape(nb, Q_BLOCK)
    kpos = jnp.arange(s)
    o = lax.map(lambda a: diff_attend(a[0], k, v, a[1], kpos, lam), (qb, qpos))
    return o.transpose(1, 0, 2, 3, 4).reshape(b, s, N_HEADS, V_DIM)


def multiscale_pool(u_ext, pos, w_pool, pool_scale):
    b, n, c = u_ext.shape
    q_len = n - POOL_STATE
    uf = u_ext.astype(jnp.float32)
    cs0 = jnp.concatenate([jnp.zeros((b, 1, c), jnp.float32), jnp.cumsum(uf, axis=1)], axis=1)
    end = cs0[:, POOL_STATE + 1:POOL_STATE + 1 + q_len]
    u_new = uf[:, POOL_STATE:]
    diffs = []
    for g, w in enumerate(POOL_WINDOWS):
        sl = slice(g * POOL_GROUP, (g + 1) * POOL_GROUP)
        start = cs0[:, POOL_STATE + 1 - w:POOL_STATE + 1 - w + q_len, sl]
        count = jnp.minimum(pos + 1, w).astype(jnp.float32)[None, :, None]
        diffs.append((end[..., sl] - start) / count - u_new[..., sl])
    d = jnp.stack(diffs, axis=2)
    y = jnp.einsum('bqgc,gcd->bqgd', d, w_pool.astype(jnp.float32)).reshape(b, q_len, c)
    return (y * pool_scale.astype(jnp.float32)).astype(u_ext.dtype)


def merge_groups(o_att, o_pool, subln, lam_init, w_out):
    b, q_len = o_att.shape[:2]
    a = (rmsnorm(o_att, subln) * (1.0 - lam_init)).reshape(b, q_len, ATTN_WIDTH).astype(o_pool.dtype)
    return jnp.concatenate([a, o_pool], axis=-1) @ w_out


def moe_ffn(h, w_router, b_router, w_gate, b_gate, w_up, b_up, w_down, b_down):
    t, d = h.shape
    logits = h.astype(jnp.float32) @ w_router.astype(jnp.float32) + b_router.astype(jnp.float32)
    top_val, top_idx = lax.top_k(logits, TOP_K)
    gates = jax.nn.softmax(top_val, axis=-1)
    n_assign = t * TOP_K
    flat_e = top_idx.reshape(-1)
    flat_tok = jnp.arange(n_assign, dtype=jnp.int32) // TOP_K
    flat_g = gates.reshape(-1)
    order = jnp.argsort(flat_e)
    sorted_e = flat_e[order]
    counts = jnp.zeros((N_EXPERTS,), jnp.int32).at[flat_e].add(1)
    padded = (counts + MOE_BLOCK - 1) // MOE_BLOCK * MOE_BLOCK
    pad_end = jnp.cumsum(padded)
    pad_start = pad_end - padded
    start = jnp.cumsum(counts) - counts
    dest = pad_start[sorted_e] + (jnp.arange(n_assign, dtype=jnp.int32) - start[sorted_e])
    n_blocks = -(-n_assign // MOE_BLOCK) + N_EXPERTS
    n_pad = n_blocks * MOE_BLOCK
    buf_tok = jnp.full((n_pad,), t, jnp.int32).at[dest].set(flat_tok[order])
    buf_gate = jnp.zeros((n_pad,), jnp.float32).at[dest].set(flat_g[order])
    block_e = jnp.minimum(jnp.searchsorted(pad_end, jnp.arange(n_blocks, dtype=jnp.int32) * MOE_BLOCK, side='right'), N_EXPERTS - 1)
    h_pad = jnp.concatenate([h, jnp.zeros((1, d), h.dtype)], axis=0)

    def run_block(args):
        tok, e = args
        xb = h_pad[tok]
        g = jnp.minimum(xb @ w_gate[e] + b_gate[e], SWIGLU_LIMIT)
        u = jnp.clip(xb @ w_up[e] + b_up[e], -SWIGLU_LIMIT, SWIGLU_LIMIT)
        a = (u + 1.0) * g * jax.nn.sigmoid(SWIGLU_ALPHA * g)
        return a @ w_down[e] + b_down[e]

    out = lax.map(run_block, (buf_tok.reshape(n_blocks, MOE_BLOCK), block_e))
    out = out.reshape(n_pad, d).astype(jnp.float32) * buf_gate[:, None]
    y = jax.ops.segment_sum(out, buf_tok, num_segments=t + 1)[:t]
    return y.astype(h.dtype)


def setup_inputs(seed: int = 0) -> dict:
    key = jax.random.key(seed)
    ks = jax.random.split(key, 32)
    f32 = jnp.float32
    nrm = lambda k, shape, s: jax.random.normal(k, shape, f32) * s
    L = DEPTH
    return {
        'x_prompt': nrm(ks[0], (BATCH, SEQ, D_MODEL), 1.0),
        'x_sample': nrm(ks[1], (DEC_BATCH, DEC_SEQ, D_MODEL), 1.0),
        'cache_k': nrm(ks[2], (L, DEC_BATCH, PAST_LEN, N_HEADS, 2, QK_DIM), 1.0),
        'cache_v': nrm(ks[3], (L, DEC_BATCH, PAST_LEN, N_HEADS, V_DIM), 1.0),
        'state_pool': nrm(ks[4], (L, DEC_BATCH, POOL_STATE, POOL_WIDTH), 1.0),
        'attn_norm': 1.0 + nrm(ks[5], (L, D_MODEL), 0.02),
        'w_in': nrm(ks[6], (L, D_MODEL, IN_WIDTH), D_MODEL ** -0.5),
        'q_norm': 1.0 + nrm(ks[7], (L, QK_DIM), 0.02),
        'k_norm': 1.0 + nrm(ks[8], (L, QK_DIM), 0.02),
        'lambda_q1': nrm(ks[9], (L, QK_DIM), 0.1),
        'lambda_k1': nrm(ks[10], (L, QK_DIM), 0.1),
        'lambda_q2': nrm(ks[11], (L, QK_DIM), 0.1),
        'lambda_k2': nrm(ks[12], (L, QK_DIM), 0.1),
        'subln': 1.0 + nrm(ks[13], (L, V_DIM), 0.02),
        'w_pool': nrm(ks[14], (L, N_POOL_GROUPS, POOL_GROUP, POOL_GROUP), POOL_GROUP ** -0.5),
        'pool_scale': 1.0 + nrm(ks[15], (L, POOL_WIDTH), 0.1),
        'w_out': nrm(ks[16], (L, D_MODEL, D_MODEL), D_MODEL ** -0.5),
        'ffn_norm': 1.0 + nrm(ks[17], (L, D_MODEL), 0.02),
        'w_router': nrm(ks[18], (L, D_MODEL, N_EXPERTS), D_MODEL ** -0.5),
        'b_router': nrm(ks[19], (L, N_EXPERTS), 0.01),
        'w_gate': nrm(ks[20], (L, N_EXPERTS, D_MODEL, D_FF), D_MODEL ** -0.5),
        'b_gate': nrm(ks[21], (L, N_EXPERTS, D_FF), 0.02),
        'w_up': nrm(ks[22], (L, N_EXPERTS, D_MODEL, D_FF), D_MODEL ** -0.5),
        'b_up': nrm(ks[23], (L, N_EXPERTS, D_FF), 0.02),
        'w_down': nrm(ks[24], (L, N_EXPERTS, D_FF, D_MODEL), D_FF ** -0.5),
        'b_down': nrm(ks[25], (L, N_EXPERTS, D_MODEL), 0.02),
    }


def reference(x_prompt, x_sample, cache_k, cache_v, state_pool, attn_norm, w_in, q_norm, k_norm,
              lambda_q1, lambda_k1, lambda_q2, lambda_k2, subln, w_pool, pool_scale, w_out,
              ffn_norm, w_router, b_router, w_gate, b_gate, w_up, b_up, w_down, b_down):
    xp, xs = x_prompt, x_sample
    bp, sp, d = xp.shape
    bs, ss, _ = xs.shape
    past = cache_k.shape[2]
    kp_l, vp_l, pp_l, ks_l, vs_l, ps_l = [], [], [], [], [], []
    for l in range(DEPTH):
        lam_init = 0.8 - 0.6 * math.exp(-0.3 * l)
        lam = (jnp.exp(jnp.sum(lambda_q1[l].astype(jnp.float32) * lambda_k1[l].astype(jnp.float32)))
               - jnp.exp(jnp.sum(lambda_q2[l].astype(jnp.float32) * lambda_k2[l].astype(jnp.float32)))
               + lam_init)
        moe_w = (w_router[l], b_router[l], w_gate[l], b_gate[l], w_up[l], b_up[l], w_down[l], b_down[l])

        hn = rmsnorm(xp, attn_norm[l])
        q, k, v, u = project(hn, w_in[l], q_norm[l], k_norm[l])
        o_att = prompt_attention(q, k, v, lam)
        u_ext = jnp.concatenate([jnp.zeros((bp, POOL_STATE, POOL_WIDTH), u.dtype), u], axis=1)
        o_pool = multiscale_pool(u_ext, jnp.arange(sp), w_pool[l], pool_scale[l])
        xp = xp + merge_groups(o_att, o_pool, subln[l], lam_init, w_out[l])
        xp = xp + moe_ffn(rmsnorm(xp, ffn_norm[l]).reshape(-1, d), *moe_w).reshape(bp, sp, d)
        kp_l.append(k)
        vp_l.append(v)
        pp_l.append(u_ext[:, -POOL_STATE:])

        hn = rmsnorm(xs, attn_norm[l])
        q, k, v, u = project(hn, w_in[l], q_norm[l], k_norm[l])
        k_all = jnp.concatenate([cache_k[l], k], axis=1)
        v_all = jnp.concatenate([cache_v[l], v], axis=1)
        q_pos = past + jnp.arange(ss)
        o_att = diff_attend(q, k_all, v_all, q_pos, jnp.arange(past + ss), lam)
        u_ext = jnp.concatenate([state_pool[l], u], axis=1)
        o_pool = multiscale_pool(u_ext, q_pos, w_pool[l], pool_scale[l])
        xs = xs + merge_groups(o_att, o_pool, subln[l], lam_init, w_out[l])
        xs = xs + moe_ffn(rmsnorm(xs, ffn_norm[l]).reshape(-1, d), *moe_w).reshape(bs, ss, d)
        ks_l.append(k)
        vs_l.append(v)
        ps_l.append(u_ext[:, -POOL_STATE:])

    return (xp, xs, jnp.stack(kp_l), jnp.stack(vp_l), jnp.stack(pp_l),
            jnp.stack(ks_l), jnp.stack(vs_l), jnp.stack(ps_l))
```

```python
import functools
import math

import jax
import jax.numpy as jnp
from jax import lax
from jax.experimental import pallas as pl
from jax.experimental.pallas import tpu as pltpu

CHUNK = 64
N_HEADS = 4
QK_DIM = 64
V_DIM = 128
HEAD_W = 2 * QK_DIM
QK_WIDTH = N_HEADS * HEAD_W
ATTN_WIDTH = N_HEADS * V_DIM
POOL_WINDOWS = (2, 4, 8, 16)
POOL_GROUP = 128
POOL_WIDTH = POOL_GROUP * len(POOL_WINDOWS)
POOL_STATE = max(POOL_WINDOWS) - 1
HALO = POOL_STATE + 1
N_EXPERTS = 32
TOP_K = 4
SWIGLU_LIMIT = 7.0
SWIGLU_ALPHA = 1.702
EPS = 1e-6

ROW_TILE = 512
ATT_BLOCK = 256
MOE_TILE = 512
VMEM_LIMIT = 48 * 1024 * 1024


def _pick_tile(n, cap, mult=8):
    t = min(n, cap)
    while n % t or t % mult:
        t -= 1
    return t


def _split_bf16(x):
    hi = x.astype(jnp.bfloat16)
    lo = (x - hi.astype(jnp.float32)).astype(jnp.bfloat16)
    return hi, lo


def _dot_nt(a, b):
    return lax.dot_general(a, b, (((1,), (1,)), ((), ())), preferred_element_type=jnp.float32)


def _dot(a, b):
    return jnp.dot(a, b, preferred_element_type=jnp.float32)


def _in_proj_kernel(x_ref, g_ref, w_ref, qg_ref, kg_ref, seg_ref,
                    q_ref, k_ref, kb_ref, v_ref, vb_ref, u_ref):
    x = x_ref[...]
    ms = jnp.mean(x * x, axis=-1, keepdims=True)
    hn = (x * lax.rsqrt(ms + EPS) * g_ref[...]).astype(jnp.bfloat16)
    z = _dot(hn, w_ref[...])
    seg = seg_ref[...]

    def qk_norm(t, gain):
        hi, lo = _split_bf16(t * t)
        ss = _dot(hi, seg) + _dot(lo, seg)
        return t * lax.rsqrt(ss * (1.0 / QK_DIM) + EPS) * gain

    qn = qk_norm(z[:, :QK_WIDTH], qg_ref[...])
    kn = qk_norm(z[:, QK_WIDTH:2 * QK_WIDTH], kg_ref[...])
    v = z[:, 2 * QK_WIDTH:2 * QK_WIDTH + ATTN_WIDTH]
    q_ref[...] = (qn * (QK_DIM ** -0.5)).astype(jnp.bfloat16)
    k_ref[...] = kn
    kb_ref[...] = kn.astype(jnp.bfloat16)
    v_ref[...] = v
    vb_ref[...] = v.astype(jnp.bfloat16)
    u_ref[...] = z[:, 2 * QK_WIDTH + ATTN_WIDTH:]


def _in_proj(x, g, w_bf, qg, kg, seg):
    t, d = x.shape
    tm = _pick_tile(t, ROW_TILE)
    row = lambda w: pl.BlockSpec((tm, w), lambda i: (i, 0))
    full = lambda a: pl.BlockSpec(a.shape, lambda i: (0, 0))
    f32, bf16 = jnp.float32, jnp.bfloat16
    return pl.pallas_call(
        _in_proj_kernel,
        grid=(t // tm,),
        in_specs=[row(d), full(g), full(w_bf), full(qg), full(kg), full(seg)],
        out_specs=[row(QK_WIDTH)] * 3 + [row(ATTN_WIDTH)] * 2 + [row(POOL_WIDTH)],
        out_shape=[jax.ShapeDtypeStruct((t, QK_WIDTH), bf16),
                   jax.ShapeDtypeStruct((t, QK_WIDTH), f32),
                   jax.ShapeDtypeStruct((t, QK_WIDTH), bf16),
                   jax.ShapeDtypeStruct((t, ATTN_WIDTH), f32),
                   jax.ShapeDtypeStruct((t, ATTN_WIDTH), bf16),
                   jax.ShapeDtypeStruct((t, POOL_WIDTH), f32)],
        compiler_params=pltpu.CompilerParams(dimension_semantics=("arbitrary",),
                                             vmem_limit_bytes=VMEM_LIMIT),
        name="in_proj",
    )(x, g, w_bf, qg, kg, seg)


def _lambda_scalar(lam_ref, lam_init):
    l = lam_ref[...]
    a = jnp.sum(l[0:1] * l[1:2], axis=-1, keepdims=True)
    b = jnp.sum(l[2:3] * l[3:4], axis=-1, keepdims=True)
    return jnp.exp(a) - jnp.exp(b) + lam_init


def _split_maps(q):
    lane = lax.broadcasted_iota(jnp.int32, q.shape, 1)
    zero = jnp.zeros_like(q)
    return jnp.where(lane < QK_DIM, q, zero), jnp.where(lane >= QK_DIM, q, zero)


def _sub_norm(o, subln, lam_init):
    ms = jnp.mean(o * o, axis=-1, keepdims=True)
    return o * lax.rsqrt(ms + EPS) * subln * (1.0 - lam_init)


def _online_update(s, vb, m, l, acc):
    m_new = jnp.maximum(m, jnp.max(s, axis=-1, keepdims=True))
    alpha = jnp.exp(m - m_new)
    p = jnp.exp(s - m_new)
    l_new = alpha * l + jnp.sum(p, axis=-1, keepdims=True)
    acc_new = alpha * acc + _dot(p.astype(jnp.bfloat16), vb)
    return m_new, l_new, acc_new


def _prompt_attn_kernel(lam_ref, subln_ref, q_ref, k_ref, v_ref, o_ref, *, lam_init, blk):
    i = pl.program_id(2)
    lam = _lambda_scalar(lam_ref, lam_init)
    q1, q2 = _split_maps(q_ref[...])

    def step(j, carry, masked):
        m1, l1, a1, m2, l2, a2 = carry
        off = pl.multiple_of(j * blk, blk)
        kb = k_ref[pl.ds(off, blk), :]
        vb = v_ref[pl.ds(off, blk), :]
        s1 = _dot_nt(q1, kb)
        s2 = _dot_nt(q2, kb)
        if masked:
            r = lax.broadcasted_iota(jnp.int32, s1.shape, 0) // CHUNK
            c = lax.broadcasted_iota(jnp.int32, s1.shape, 1) // CHUNK
            vis = c <= r
            s1 = jnp.where(vis, s1, -jnp.inf)
            s2 = jnp.where(vis, s2, -jnp.inf)
        m1, l1, a1 = _online_update(s1, vb, m1, l1, a1)
        m2, l2, a2 = _online_update(s2, vb, m2, l2, a2)
        return m1, l1, a1, m2, l2, a2

    neg = jnp.full((blk, 1), -jnp.inf, jnp.float32)
    zl = jnp.zeros((blk, 1), jnp.float32)
    za = jnp.zeros((blk, V_DIM), jnp.float32)
    carry = lax.fori_loop(0, i, lambda j, c: step(j, c, False), (neg, zl, za, neg, zl, za))
    m1, l1, a1, m2, l2, a2 = step(i, carry, True)
    o = a1 / l1 - lam * (a2 / l2)
    o_ref[...] = _sub_norm(o, subln_ref[...], lam_init).astype(o_ref.dtype)


def _prompt_attention(q, k, v, lam_rows, subln, batch, seq, lam_init):
    blk = _pick_tile(seq, ATT_BLOCK, CHUNK)
    nq = seq // blk
    t = batch * seq
    kern = functools.partial(_prompt_attn_kernel, lam_init=lam_init, blk=blk)
    return pl.pallas_call(
        kern,
        grid=(batch, N_HEADS, nq),
        in_specs=[pl.BlockSpec(lam_rows.shape, lambda b, h, i: (0, 0)),
                  pl.BlockSpec(subln.shape, lambda b, h, i: (0, 0)),
                  pl.BlockSpec((blk, HEAD_W), lambda b, h, i: (b * nq + i, h)),
                  pl.BlockSpec((seq, HEAD_W), lambda b, h, i: (b, h)),
                  pl.BlockSpec((seq, V_DIM), lambda b, h, i: (b, h))],
        out_specs=pl.BlockSpec((blk, V_DIM), lambda b, h, i: (b * nq + i, h)),
        out_shape=jax.ShapeDtypeStruct((t, ATTN_WIDTH), jnp.bfloat16),
        compiler_params=pltpu.CompilerParams(dimension_semantics=("arbitrary",) * 3,
                                             vmem_limit_bytes=VMEM_LIMIT),
        name="prompt_attention",
    )(lam_rows, subln, q, k, v)


def _sample_attn_kernel(lam_ref, subln_ref, q_ref, kn_ref, vn_ref, ck_ref, cv_ref, o_ref, *, lam_init, past):
    lam = _lambda_scalar(lam_ref, lam_init)
    q = q_ref[...]
    nq = q.shape[0]
    qpos = past + lax.broadcasted_iota(jnp.int32, (nq, nq), 0)
    kpos = past + lax.broadcasted_iota(jnp.int32, (nq, nq), 1)
    vis_new = (kpos // CHUNK) <= (qpos // CHUNK)
    for h in range(N_HEADS):
        qk = slice(h * HEAD_W, (h + 1) * HEAD_W)
        vv = slice(h * V_DIM, (h + 1) * V_DIM)
        q1, q2 = _split_maps(q[:, qk])
        kc = ck_ref[0, :, qk].astype(jnp.bfloat16)
        vc = cv_ref[0, :, vv].astype(jnp.bfloat16)
        kn = kn_ref[:, qk]
        vn = vn_ref[:, vv]

        def one_map(qm):
            sc = _dot_nt(qm, kc)
            sn = jnp.where(vis_new, _dot_nt(qm, kn), -jnp.inf)
            m = jnp.maximum(jnp.max(sc, axis=-1, keepdims=True), jnp.max(sn, axis=-1, keepdims=True))
            pc = jnp.exp(sc - m)
            pn = jnp.exp(sn - m)
            l = jnp.sum(pc, axis=-1, keepdims=True) + jnp.sum(pn, axis=-1, keepdims=True)
            return (_dot(pc.astype(jnp.bfloat16), vc) + _dot(pn.astype(jnp.bfloat16), vn)) / l

        o = one_map(q1) - lam * one_map(q2)
        o_ref[:, vv] = _sub_norm(o, subln_ref[...], lam_init).astype(o_ref.dtype)


def _sample_attention(q, k_new, v_new, cache_k, cache_v, lam_rows, subln, batch, nq, lam_init):
    past = cache_k.shape[1]
    kern = functools.partial(_sample_attn_kernel, lam_init=lam_init, past=past)
    row = lambda w: pl.BlockSpec((nq, w), lambda b: (b, 0))
    return pl.pallas_call(
        kern,
        grid=(batch,),
        in_specs=[pl.BlockSpec(lam_rows.shape, lambda b: (0, 0)),
                  pl.BlockSpec(subln.shape, lambda b: (0, 0)),
                  row(QK_WIDTH), row(QK_WIDTH), row(ATTN_WIDTH),
                  pl.BlockSpec((1, past, QK_WIDTH), lambda b: (b, 0, 0)),
                  pl.BlockSpec((1, past, ATTN_WIDTH), lambda b: (b, 0, 0))],
        out_specs=row(ATTN_WIDTH),
        out_shape=jax.ShapeDtypeStruct((batch * nq, ATTN_WIDTH), jnp.bfloat16),
        compiler_params=pltpu.CompilerParams(dimension_semantics=("arbitrary",),
                                             vmem_limit_bytes=VMEM_LIMIT),
        name="sample_attention",
    )(lam_rows, subln, q, k_new, v_new, cache_k, cache_v)


def _pool_kernel(u_ref, prev_ref, hist_ref, w_ref, scale_ref, o_ref, ext_ref, *, pos0, tq):
    j = pl.program_id(1)
    ext_ref[0:HALO, :] = jnp.where(j == 0, hist_ref[0], prev_ref[0])
    ext_ref[HALO:, :] = u_ref[0]
    pos = pos0 + j * tq + lax.broadcasted_iota(jnp.int32, (tq, 1), 0)
    for g, w in enumerate(POOL_WINDOWS):
        cols = slice(g * POOL_GROUP, (g + 1) * POOL_GROUP)
        acc = ext_ref[HALO:, cols]
        for back in range(1, w):
            acc = acc + ext_ref[HALO - back:HALO - back + tq, cols]
        count = jnp.minimum(pos + 1, w).astype(jnp.float32)
        d = acc / count - ext_ref[HALO:, cols]
        y = _dot(d.astype(jnp.bfloat16), w_ref[g])
        o_ref[0, :, cols] = (y * scale_ref[:, cols]).astype(o_ref.dtype)


def _pool(u, hist, w_pool_bf, scale, pos0):
    b, q, c = u.shape
    tq = _pick_tile(q, ROW_TILE, HALO)
    per = tq // HALO
    kern = functools.partial(_pool_kernel, pos0=pos0, tq=tq)
    return pl.pallas_call(
        kern,
        grid=(b, q // tq),
        in_specs=[pl.BlockSpec((1, tq, c), lambda i, j: (i, j, 0)),
                  pl.BlockSpec((1, HALO, c), lambda i, j: (i, jnp.maximum(j * per - 1, 0), 0)),
                  pl.BlockSpec((1, HALO, c), lambda i, j: (i, 0, 0)),
                  pl.BlockSpec(w_pool_bf.shape, lambda i, j: (0, 0, 0)),
                  pl.BlockSpec(scale.shape, lambda i, j: (0, 0))],
        out_specs=pl.BlockSpec((1, tq, c), lambda i, j: (i, j, 0)),
        out_shape=jax.ShapeDtypeStruct((b, q, c), jnp.bfloat16),
        scratch_shapes=[pltpu.VMEM((HALO + tq, c), jnp.float32)],
        compiler_params=pltpu.CompilerParams(dimension_semantics=("arbitrary",) * 2,
                                             vmem_limit_bytes=VMEM_LIMIT),
        name="pool",
    )(u, u, hist, w_pool_bf, scale)


def _out_proj_kernel(a_ref, p_ref, x_ref, wa_ref, wp_ref, g_ref, wr_hi_ref, wr_lo_ref, br_ref,
                     tri_ref, cin_ref,
                     x1_ref, h_ref, idx_ref, gate_ref, rank_ref, cout_ref, carry_ref):
    i = pl.program_id(0)

    @pl.when(i == 0)
    def _():
        carry_ref[...] = cin_ref[...]

    x1 = x_ref[...] + _dot(a_ref[...], wa_ref[...]) + _dot(p_ref[...], wp_ref[...])
    x1_ref[...] = x1
    ms = jnp.mean(x1 * x1, axis=-1, keepdims=True)
    h = x1 * lax.rsqrt(ms + EPS) * g_ref[...]
    h_ref[...] = h.astype(jnp.bfloat16)

    h_hi, h_lo = _split_bf16(h)
    wr_hi = wr_hi_ref[...]
    logits = _dot_nt(wr_hi, h_hi) + _dot_nt(wr_hi, h_lo) + _dot_nt(wr_lo_ref[...], h_hi) + br_ref[...]
    expert = lax.broadcasted_iota(jnp.int32, logits.shape, 0)
    work = logits
    vals, idxs, hots = [], [], []
    for _ in range(TOP_K):
        m = jnp.max(work, axis=0, keepdims=True)
        sel = jnp.min(jnp.where(work == m, expert, N_EXPERTS), axis=0, keepdims=True)
        hot = expert == sel
        work = jnp.where(hot, -jnp.inf, work)
        vals.append(m)
        idxs.append(sel)
        hots.append(hot)
    ex = [jnp.exp(v - vals[0]) for v in vals]
    den = ex[0] + ex[1] + ex[2] + ex[3]
    gate_ref[...] = jnp.concatenate([e / den for e in ex], axis=0)
    idx_ref[...] = jnp.concatenate(idxs, axis=0)

    hot_all = hots[0] | hots[1] | hots[2] | hots[3]
    hot_f = jnp.where(hot_all, 1.0, 0.0)
    prefix = _dot(hot_f.astype(jnp.bfloat16), tri_ref[...]) + carry_ref[:, 0:1]
    ranks = [jnp.sum(jnp.where(hot, prefix, 0.0), axis=0, keepdims=True) for hot in hots]
    rank_ref[...] = jnp.concatenate(ranks, axis=0).astype(jnp.int32)
    carry_ref[...] = carry_ref[...] + jnp.sum(hot_f, axis=1, keepdims=True)
    cout_ref[...] = carry_ref[...]


def _out_proj(a, p, x, wa, wp, g, wr_hi, wr_lo, br, tri, carry_in):
    t, d = x.shape
    tm = tri.shape[0]
    row = lambda w: pl.BlockSpec((tm, w), lambda i: (i, 0))
    col = pl.BlockSpec((TOP_K, tm), lambda i: (0, i))
    full = lambda arr: pl.BlockSpec(arr.shape, lambda i: (0, 0))
    f32, i32 = jnp.float32, jnp.int32
    return pl.pallas_call(
        _out_proj_kernel,
        grid=(t // tm,),
        in_specs=[row(ATTN_WIDTH), row(POOL_WIDTH), row(d), full(wa), full(wp), full(g),
                  full(wr_hi), full(wr_lo), full(br), full(tri), full(carry_in)],
        out_specs=[row(d), row(d), col, col, col, full(carry_in)],
        out_shape=[jax.ShapeDtypeStruct((t, d), f32),
                   jax.ShapeDtypeStruct((t, d), jnp.bfloat16),
                   jax.ShapeDtypeStruct((TOP_K, t), i32),
                   jax.ShapeDtypeStruct((TOP_K, t), f32),
                   jax.ShapeDtypeStruct((TOP_K, t), i32),
                   jax.ShapeDtypeStruct(carry_in.shape, f32)],
        scratch_shapes=[pltpu.VMEM(carry_in.shape, f32)],
        compiler_params=pltpu.CompilerParams(dimension_semantics=("arbitrary",),
                                             vmem_limit_bytes=VMEM_LIMIT),
        name="out_proj_router",
    )(a, p, x, wa, wp, g, wr_hi, wr_lo, br, tri, carry_in)


def _expert_kernel(be_ref, nu_ref, x_ref, wg_ref, bg_ref, wu_ref, bu_ref, wd_ref, bd_ref, o_ref):
    @pl.when(pl.program_id(0) < nu_ref[0])
    def _():
        x = x_ref[...]
        g = jnp.minimum(_dot(x, wg_ref[0]) + bg_ref[0], SWIGLU_LIMIT)
        u = jnp.clip(_dot(x, wu_ref[0]) + bu_ref[0], -SWIGLU_LIMIT, SWIGLU_LIMIT)
        a = (u + 1.0) * g * jax.nn.sigmoid(SWIGLU_ALPHA * g)
        o_ref[...] = _dot(a.astype(jnp.bfloat16), wd_ref[0]) + bd_ref[0]


def _experts(block_e, n_used, x_sorted, wg, bg, wu, bu, wd, bd):
    n_pad, d = x_sorted.shape
    f = wg.shape[2]
    tm = MOE_TILE
    rows = lambda i, be, nu: (jnp.minimum(i, nu[0] - 1), 0)
    wsel = lambda i, be, nu: (be[i], 0, 0)
    return pl.pallas_call(
        _expert_kernel,
        grid_spec=pltpu.PrefetchScalarGridSpec(
            num_scalar_prefetch=2,
            grid=(n_pad // tm,),
            in_specs=[pl.BlockSpec((tm, d), rows),
                      pl.BlockSpec((1, d, f), wsel), pl.BlockSpec((1, 1, f), wsel),
                      pl.BlockSpec((1, d, f), wsel), pl.BlockSpec((1, 1, f), wsel),
                      pl.BlockSpec((1, f, d), wsel), pl.BlockSpec((1, 1, d), wsel)],
            out_specs=pl.BlockSpec((tm, d), rows)),
        out_shape=jax.ShapeDtypeStruct((n_pad, d), jnp.float32),
        compiler_params=pltpu.CompilerParams(dimension_semantics=("arbitrary",),
                                             vmem_limit_bytes=VMEM_LIMIT),
        name="experts",
    )(block_e, n_used, x_sorted, wg, bg, wu, bu, wd, bd)


def _moe(h_all, idx, gate, rank, counts, x1_all, wg, bg, wu, bu, wd, bd):
    t, d = h_all.shape
    tm = MOE_TILE
    cnt = counts.astype(jnp.int32)
    padded = (cnt + tm - 1) // tm * tm
    pad_end = jnp.cumsum(padded)
    pad_start = pad_end - padded
    dest = pad_start[idx] + rank
    n_blocks = -(-(t * TOP_K) // tm) + N_EXPERTS
    n_pad = n_blocks * tm
    tok = jnp.broadcast_to(jnp.arange(t, dtype=jnp.int32)[None, :], (TOP_K, t))
    src = jnp.zeros((n_pad,), jnp.int32).at[dest.reshape(-1)].set(tok.reshape(-1))
    block_e = jnp.minimum(
        jnp.searchsorted(pad_end, jnp.arange(n_blocks, dtype=jnp.int32) * tm, side='right'),
        N_EXPERTS - 1).astype(jnp.int32)
    n_used = (pad_end[-1:] // tm).astype(jnp.int32)
    x_sorted = h_all[src]
    out_sorted = _experts(block_e, n_used, x_sorted, wg, bg, wu, bu, wd, bd)
    y = x1_all
    for k in range(TOP_K):
        y = y + gate[k][:, None] * out_sorted[dest[k]]
    return y


def kernel(x_prompt, x_sample, cache_k, cache_v, state_pool, attn_norm, w_in, q_norm, k_norm,
           lambda_q1, lambda_k1, lambda_q2, lambda_k2, subln, w_pool, pool_scale, w_out,
           ffn_norm, w_router, b_router, w_gate, b_gate, w_up, b_up, w_down, b_down):
    f32, bf16 = jnp.float32, jnp.bfloat16
    bp, sp, d = x_prompt.shape
    bs, ss, _ = x_sample.shape
    depth = w_in.shape[0]
    past = cache_k.shape[2]
    tp, ts = bp * sp, bs * ss
    xp = x_prompt.reshape(tp, d)
    xs = x_sample.reshape(ts, d)

    tm_p = _pick_tile(tp, ROW_TILE)
    tm_s = _pick_tile(ts, ROW_TILE)
    group = lax.broadcasted_iota(jnp.int32, (QK_WIDTH, QK_WIDTH), 0) // QK_DIM
    seg = (group == group.T).astype(bf16)
    tri = {tm: (lax.broadcasted_iota(jnp.int32, (tm, tm), 0)
                < lax.broadcasted_iota(jnp.int32, (tm, tm), 1)).astype(bf16) for tm in {tm_p, tm_s}}

    kp_l, vp_l, pp_l, ks_l, vs_l, ps_l = [], [], [], [], [], []
    for l in range(depth):
        lam_init = 0.8 - 0.6 * math.exp(-0.3 * l)
        lam_rows = jnp.stack([lambda_q1[l], lambda_k1[l], lambda_q2[l], lambda_k2[l]]).astype(f32)
        g_attn = attn_norm[l].reshape(1, d)
        w_in_bf = w_in[l].astype(bf16)
        qg = jnp.tile(q_norm[l], QK_WIDTH // QK_DIM).reshape(1, QK_WIDTH)
        kg = jnp.tile(k_norm[l], QK_WIDTH // QK_DIM).reshape(1, QK_WIDTH)
        sub = subln[l].reshape(1, V_DIM)
        w_pool_bf = w_pool[l].astype(bf16)
        scale = pool_scale[l].reshape(1, POOL_WIDTH)
        wa = w_out[l][:ATTN_WIDTH].astype(bf16)
        wp = w_out[l][ATTN_WIDTH:].astype(bf16)
        g_ffn = ffn_norm[l].reshape(1, d)
        wr_hi, wr_lo = _split_bf16(w_router[l].T.astype(f32))
        br = b_router[l].reshape(N_EXPERTS, 1).astype(f32)
        wg, wu, wd = w_gate[l].astype(bf16), w_up[l].astype(bf16), w_down[l].astype(bf16)
        bg = b_gate[l].reshape(N_EXPERTS, 1, -1)
        bu = b_up[l].reshape(N_EXPERTS, 1, -1)
        bd = b_down[l].reshape(N_EXPERTS, 1, -1)

        q, k, kb, v, vb, u = _in_proj(xp, g_attn, w_in_bf, qg, kg, seg)
        a = _prompt_attention(q, kb, vb, lam_rows, sub, bp, sp, lam_init)
        u3 = u.reshape(bp, sp, POOL_WIDTH)
        o_pool = _pool(u3, jnp.zeros((bp, HALO, POOL_WIDTH), f32), w_pool_bf, scale, 0)
        carry0 = jnp.zeros((N_EXPERTS, 128), f32)
        x1p, hp, idx_p, gate_p, rank_p, carry1 = _out_proj(
            a, o_pool.reshape(tp, POOL_WIDTH), xp, wa, wp, g_ffn, wr_hi, wr_lo, br, tri[tm_p], carry0)
        kp_l.append(k.reshape(bp, sp, N_HEADS, 2, QK_DIM))
        vp_l.append(v.reshape(bp, sp, N_HEADS, V_DIM))
        pp_l.append(u3[:, sp - POOL_STATE:])

        q, k, kb, v, vb, u = _in_proj(xs, g_attn, w_in_bf, qg, kg, seg)
        a = _sample_attention(q, kb, vb, cache_k[l].reshape(bs, past, QK_WIDTH),
                              cache_v[l].reshape(bs, past, ATTN_WIDTH), lam_rows, sub, bs, ss, lam_init)
        u3 = u.reshape(bs, ss, POOL_WIDTH)
        hist = jnp.concatenate([jnp.zeros((bs, 1, POOL_WIDTH), f32), state_pool[l]], axis=1)
        o_pool = _pool(u3, hist, w_pool_bf, scale, past)
        x1s, hs, idx_s, gate_s, rank_s, carry2 = _out_proj(
            a, o_pool.reshape(ts, POOL_WIDTH), xs, wa, wp, g_ffn, wr_hi, wr_lo, br, tri[tm_s], carry1)
        ks_l.append(k.reshape(bs, ss, N_HEADS, 2, QK_DIM))
        vs_l.append(v.reshape(bs, ss, N_HEADS, V_DIM))
        ps_l.append(jnp.concatenate([state_pool[l], u3], axis=1)[:, -POOL_STATE:])

        cat = lambda p_, s_: jnp.concatenate([p_, s_], axis=-1 if p_.shape[0] == TOP_K else 0)
        y = _moe(cat(hp, hs), cat(idx_p, idx_s), cat(gate_p, gate_s), cat(rank_p, rank_s),
                 carry2[:, 0], cat(x1p, x1s), wg, bg, wu, bu, wd, bd)
        xp, xs = y[:tp], y[tp:]

    return (xp.reshape(bp, sp, d), xs.reshape(bs, ss, d), jnp.stack(kp_l), jnp.stack(vp_l),
            jnp.stack(pp_l), jnp.stack(ks_l), jnp.stack(vs_l), jnp.stack(ps_l))
```

```python
import functools
import math

import jax
import jax.numpy as jnp
from jax import lax
from jax.experimental import pallas as pl
from jax.experimental.pallas import tpu as pltpu

CHUNK = 64
N_HEADS = 4
QK_DIM = 64
V_DIM = 128
HEAD_W = 2 * QK_DIM
QK_WIDTH = N_HEADS * HEAD_W
ATTN_WIDTH = N_HEADS * V_DIM
POOL_WINDOWS = (2, 4, 8, 16)
POOL_GROUP = 128
POOL_WIDTH = POOL_GROUP * len(POOL_WINDOWS)
POOL_STATE = max(POOL_WINDOWS) - 1
HALO = POOL_STATE + 1
N_EXPERTS = 32
TOP_K = 4
SWIGLU_LIMIT = 7.0
SWIGLU_ALPHA = 1.702
EPS = 1e-6
LOG2E = 1.4426950408889634
D_MODEL = 1024
LANES = 128
ROW_SUB = D_MODEL // LANES

ROW_TILE = 512
ATT_BLOCK = 256
MOE_TILE = 512
DMA_UNROLL = 8
ZERO_ROWS = 64
VMEM_LIMIT = 56 * 1024 * 1024


def _pick_tile(n, cap, mult=8):
    t = min(n, cap)
    while n % t or t % mult:
        t -= 1
    return t


def _split_bf16(x):
    hi = x.astype(jnp.bfloat16)
    lo = (x - hi.astype(jnp.float32)).astype(jnp.bfloat16)
    return hi, lo


def _dot_nt(a, b):
    return lax.dot_general(a, b, (((1,), (1,)), ((), ())), preferred_element_type=jnp.float32)


def _dot(a, b):
    return jnp.dot(a, b, preferred_element_type=jnp.float32)


def _store_row_tiles(ref, x):
    m = x.shape[0]
    for c in range(ROW_SUB):
        ref[pl.ds(c, m, stride=ROW_SUB), :] = x[:, c * LANES:(c + 1) * LANES]


def _load_row_tiles(ref, m):
    return jnp.concatenate([ref[pl.ds(c, m, stride=ROW_SUB), :] for c in range(ROW_SUB)], axis=1)


def _in_proj_kernel(x_ref, g_ref, w_ref, wqt_ref, wvt_ref, qg_ref, kg_ref, seg_ref, *out_refs, transposed, blk):
    q_ref, kb_ref, vv_ref, k_ref, v_ref, u_ref = out_refs
    x = x_ref[...]
    ms = jnp.mean(x * x, axis=-1, keepdims=True)
    hn = (x * lax.rsqrt(ms + EPS) * g_ref[...]).astype(jnp.bfloat16)
    z = _dot(hn, w_ref[:, QK_WIDTH:] if transposed else w_ref[...])
    base = 0 if transposed else QK_WIDTH
    seg = seg_ref[...]

    def qk_norm(t, gain):
        hi, lo = _split_bf16(t * t)
        ss = _dot(hi, seg) + _dot(lo, seg)
        return t * lax.rsqrt(ss * (1.0 / QK_DIM) + EPS) * gain

    kn = qk_norm(z[:, base:base + QK_WIDTH], kg_ref[...])
    v = z[:, base + QK_WIDTH:base + QK_WIDTH + ATTN_WIDTH]
    k_ref[...] = kn
    kb_ref[...] = kn.astype(jnp.bfloat16)
    v_ref[...] = v
    u_ref[...] = z[:, base + QK_WIDTH + ATTN_WIDTH:]
    if transposed:
        tm = x.shape[0]
        qt = _dot_nt(wqt_ref[...], hn)
        q3 = qt.reshape(QK_WIDTH // QK_DIM, QK_DIM, tm)
        r = lax.rsqrt(jnp.mean(q3 * q3, axis=1, keepdims=True) + EPS)
        qn = (q3 * r).reshape(QK_WIDTH, tm) * (qg_ref[...] * (QK_DIM ** -0.5 * LOG2E))
        vt = _dot_nt(wvt_ref[...], hn)
        for s in range(tm // blk):
            q_ref[s] = qn[:, s * blk:(s + 1) * blk].astype(jnp.bfloat16)
            vv_ref[s] = vt[:, s * blk:(s + 1) * blk].astype(jnp.bfloat16)
    else:
        qn = qk_norm(z[:, :QK_WIDTH], qg_ref[...])
        q_ref[...] = (qn * (QK_DIM ** -0.5)).astype(jnp.bfloat16)
        vv_ref[...] = v.astype(jnp.bfloat16)


def _in_proj(x, g, w_bf, wqt, wvt, qg, kg, seg, transposed, blk):
    t, d = x.shape
    tm = _pick_tile(t, ROW_TILE, blk if transposed else 8)
    row = lambda w: pl.BlockSpec((tm, w), lambda i: (i, 0))
    full = lambda a: pl.BlockSpec(a.shape, lambda i: (0, 0))
    f32, bf16 = jnp.float32, jnp.bfloat16
    if transposed:
        slab = pl.BlockSpec((tm // blk, QK_WIDTH, blk), lambda i: (i, 0, 0))
        slab_shape = jax.ShapeDtypeStruct((t // blk, QK_WIDTH, blk), bf16)
    else:
        slab, slab_shape = row(QK_WIDTH), jax.ShapeDtypeStruct((t, QK_WIDTH), bf16)
    kern = functools.partial(_in_proj_kernel, transposed=transposed, blk=blk)
    return pl.pallas_call(
        kern,
        grid=(t // tm,),
        in_specs=[row(d), full(g), full(w_bf), full(wqt), full(wvt), full(qg), full(kg), full(seg)],
        out_specs=[slab, row(QK_WIDTH), slab, row(QK_WIDTH), row(ATTN_WIDTH), row(POOL_WIDTH)],
        out_shape=[slab_shape,
                   jax.ShapeDtypeStruct((t, QK_WIDTH), bf16),
                   slab_shape,
                   jax.ShapeDtypeStruct((t, QK_WIDTH), f32),
                   jax.ShapeDtypeStruct((t, ATTN_WIDTH), f32),
                   jax.ShapeDtypeStruct((t, POOL_WIDTH), f32)],
        compiler_params=pltpu.CompilerParams(dimension_semantics=("arbitrary",),
                                             vmem_limit_bytes=VMEM_LIMIT),
        name="in_proj_t" if transposed else "in_proj",
    )(x, g, w_bf, wqt, wvt, qg, kg, seg)


def _lambda_scalar(lam_ref, lam_init):
    l = lam_ref[...]
    a = jnp.sum(l[0:1] * l[1:2], axis=-1, keepdims=True)
    b = jnp.sum(l[2:3] * l[3:4], axis=-1, keepdims=True)
    return jnp.exp(a) - jnp.exp(b) + lam_init


def _split_maps(q, axis):
    pos = lax.broadcasted_iota(jnp.int32, q.shape, axis)
    zero = jnp.zeros_like(q)
    return jnp.where(pos < QK_DIM, q, zero), jnp.where(pos >= QK_DIM, q, zero)


def _prompt_attn_kernel(lam_ref, subln_ref, q_ref, k_ref, v_ref, o_ref,
                        s_even, s_odd, p_even, p_odd, *, lam_init, blk):
    i = pl.program_id(2)
    lam = _lambda_scalar(lam_ref, lam_init)
    q1, q2 = _split_maps(q_ref[...], 0)
    qs = (q1, q2)

    def scores(j, s_buf):
        kb = k_ref[pl.ds(pl.multiple_of(j * blk, blk), blk), :]
        for c in range(2):
            s_buf[c] = _dot(kb, qs[c])

    def stage(carry, s_cur, p_prev, v_prev, p_cur, masked, nxt):
        if nxt is not None:
            scores(*nxt)
        vb = v_ref[v_prev]
        out = []
        for c in range(2):
            m, l, acc = carry[c]
            pv = _dot(vb, p_prev[c])
            s = s_cur[c]
            if masked:
                kc = lax.broadcasted_iota(jnp.int32, s.shape, 0) // CHUNK
                qc = lax.broadcasted_iota(jnp.int32, s.shape, 1) // CHUNK
                s = jnp.where(kc <= qc, s, -jnp.inf)
            m_new = jnp.maximum(m, jnp.max(s, axis=0, keepdims=True))
            alpha = jnp.exp2(m - m_new)
            p = jnp.exp2(s - m_new)
            p_cur[c] = p.astype(jnp.bfloat16)
            out.append((m_new, alpha * l + jnp.sum(p, axis=0, keepdims=True), (acc + pv) * alpha))
        return tuple(out)

    def finish(carry, p_last):
        vb = v_ref[i]
        o = []
        for c in range(2):
            _, l, acc = carry[c]
            o.append((acc + _dot(vb, p_last[c])) / l)
        o = o[0] - lam * o[1]
        ms = jnp.mean(o * o, axis=0, keepdims=True)
        o = o * lax.rsqrt(ms + EPS) * (subln_ref[...] * (1.0 - lam_init))
        o_ref[...] = o.T.astype(o_ref.dtype)

    p_odd[...] = jnp.zeros_like(p_odd)
    scores(0, s_even)
    neg = jnp.full((1, blk), -jnp.inf, jnp.float32)
    zl = jnp.zeros((1, blk), jnp.float32)
    za = jnp.zeros((V_DIM, blk), jnp.float32)

    def pair(jj, carry):
        j = 2 * jj
        carry = stage(carry, s_even, p_odd, jnp.maximum(j - 1, 0), p_even, False, (j + 1, s_odd))
        return stage(carry, s_odd, p_even, j, p_odd, False, (j + 2, s_even))

    carry = lax.fori_loop(0, i // 2, pair, ((neg, zl, za), (neg, zl, za)))

    @pl.when(i % 2 == 0)
    def _():
        finish(stage(carry, s_even, p_odd, jnp.maximum(i - 1, 0), p_even, True, None), p_even)

    @pl.when(i % 2 == 1)
    def _():
        c = stage(carry, s_even, p_odd, jnp.maximum(i - 2, 0), p_even, False, (i, s_odd))
        finish(stage(c, s_odd, p_even, i - 1, p_odd, True, None), p_odd)


def _prompt_attention(qt, k, vt, lam_rows, subln_col, batch, seq, blk, lam_init):
    nq = seq // blk
    t = batch * seq
    kern = functools.partial(_prompt_attn_kernel, lam_init=lam_init, blk=blk)
    return pl.pallas_call(
        kern,
        grid=(batch, N_HEADS, nq),
        in_specs=[pl.BlockSpec(lam_rows.shape, lambda b, h, i: (0, 0)),
                  pl.BlockSpec(subln_col.shape, lambda b, h, i: (0, 0)),
                  pl.BlockSpec((None, HEAD_W, blk), lambda b, h, i: (b * nq + i, h, 0)),
                  pl.BlockSpec((seq, HEAD_W), lambda b, h, i: (b, h)),
                  pl.BlockSpec((nq, V_DIM, blk), lambda b, h, i: (b, h, 0))],
        out_specs=pl.BlockSpec((blk, V_DIM), lambda b, h, i: (b * nq + i, h)),
        out_shape=jax.ShapeDtypeStruct((t, ATTN_WIDTH), jnp.bfloat16),
        scratch_shapes=[pltpu.VMEM((2, blk, blk), jnp.float32), pltpu.VMEM((2, blk, blk), jnp.float32),
                        pltpu.VMEM((2, blk, blk), jnp.bfloat16), pltpu.VMEM((2, blk, blk), jnp.bfloat16)],
        compiler_params=pltpu.CompilerParams(dimension_semantics=("arbitrary",) * 3,
                                             vmem_limit_bytes=VMEM_LIMIT),
        name="prompt_attention",
    )(lam_rows, subln_col, qt, k, vt)


def _sample_attn_kernel(lam_ref, subln_ref, q_ref, kn_ref, vn_ref, ck_ref, cv_ref, o_ref, *, lam_init, past):
    lam = _lambda_scalar(lam_ref, lam_init)
    q = q_ref[...]
    nq = q.shape[0]
    qpos = past + lax.broadcasted_iota(jnp.int32, (nq, nq), 0)
    kpos = past + lax.broadcasted_iota(jnp.int32, (nq, nq), 1)
    vis_new = (kpos // CHUNK) <= (qpos // CHUNK)
    for h in range(N_HEADS):
        qk = slice(h * HEAD_W, (h + 1) * HEAD_W)
        vv = slice(h * V_DIM, (h + 1) * V_DIM)
        q1, q2 = _split_maps(q[:, qk], 1)
        kc = ck_ref[0, :, qk].astype(jnp.bfloat16)
        vc = cv_ref[0, :, vv].astype(jnp.bfloat16)
        kn = kn_ref[:, qk]
        vn = vn_ref[:, vv]

        def one_map(qm):
            sc = _dot_nt(qm, kc)
            sn = jnp.where(vis_new, _dot_nt(qm, kn), -jnp.inf)
            m = jnp.maximum(jnp.max(sc, axis=-1, keepdims=True), jnp.max(sn, axis=-1, keepdims=True))
            pc = jnp.exp(sc - m)
            pn = jnp.exp(sn - m)
            l = jnp.sum(pc, axis=-1, keepdims=True) + jnp.sum(pn, axis=-1, keepdims=True)
            return (_dot(pc.astype(jnp.bfloat16), vc) + _dot(pn.astype(jnp.bfloat16), vn)) / l

        o = one_map(q1) - lam * one_map(q2)
        ms = jnp.mean(o * o, axis=-1, keepdims=True)
        o_ref[:, vv] = (o * lax.rsqrt(ms + EPS) * subln_ref[...] * (1.0 - lam_init)).astype(o_ref.dtype)


def _sample_attention(q, k_new, v_new, cache_k, cache_v, lam_rows, subln, batch, nq, lam_init):
    past = cache_k.shape[1]
    kern = functools.partial(_sample_attn_kernel, lam_init=lam_init, past=past)
    row = lambda w: pl.BlockSpec((nq, w), lambda b: (b, 0))
    return pl.pallas_call(
        kern,
        grid=(batch,),
        in_specs=[pl.BlockSpec(lam_rows.shape, lambda b: (0, 0)),
                  pl.BlockSpec(subln.shape, lambda b: (0, 0)),
                  row(QK_WIDTH), row(QK_WIDTH), row(ATTN_WIDTH),
                  pl.BlockSpec((1, past, QK_WIDTH), lambda b: (b, 0, 0)),
                  pl.BlockSpec((1, past, ATTN_WIDTH), lambda b: (b, 0, 0))],
        out_specs=row(ATTN_WIDTH),
        out_shape=jax.ShapeDtypeStruct((batch * nq, ATTN_WIDTH), jnp.bfloat16),
        compiler_params=pltpu.CompilerParams(dimension_semantics=("arbitrary",),
                                             vmem_limit_bytes=VMEM_LIMIT),
        name="sample_attention",
    )(lam_rows, subln, q, k_new, v_new, cache_k, cache_v)


def _pool_kernel(u_ref, prev_ref, hist_ref, w_ref, scale_ref, o_ref, ext_ref, *, pos0, tq):
    j = pl.program_id(1)
    ext_ref[0:HALO, :] = jnp.where(j == 0, hist_ref[0], prev_ref[0])
    ext_ref[HALO:, :] = u_ref[0]
    pos = pos0 + j * tq + lax.broadcasted_iota(jnp.int32, (tq, 1), 0)
    for g, w in enumerate(POOL_WINDOWS):
        cols = slice(g * POOL_GROUP, (g + 1) * POOL_GROUP)
        acc = ext_ref[HALO:, cols]
        for back in range(1, w):
            acc = acc + ext_ref[HALO - back:HALO - back + tq, cols]
        count = jnp.minimum(pos + 1, w).astype(jnp.float32)
        d = acc / count - ext_ref[HALO:, cols]
        y = _dot(d.astype(jnp.bfloat16), w_ref[g])
        o_ref[0, :, cols] = (y * scale_ref[:, cols]).astype(o_ref.dtype)


def _pool(u, hist, w_pool_bf, scale, pos0):
    b, q, c = u.shape
    tq = _pick_tile(q, ROW_TILE, HALO)
    per = tq // HALO
    kern = functools.partial(_pool_kernel, pos0=pos0, tq=tq)
    return pl.pallas_call(
        kern,
        grid=(b, q // tq),
        in_specs=[pl.BlockSpec((1, tq, c), lambda i, j: (i, j, 0)),
                  pl.BlockSpec((1, HALO, c), lambda i, j: (i, jnp.maximum(j * per - 1, 0), 0)),
                  pl.BlockSpec((1, HALO, c), lambda i, j: (i, 0, 0)),
                  pl.BlockSpec(w_pool_bf.shape, lambda i, j: (0, 0, 0)),
                  pl.BlockSpec(scale.shape, lambda i, j: (0, 0))],
        out_specs=pl.BlockSpec((1, tq, c), lambda i, j: (i, j, 0)),
        out_shape=jax.ShapeDtypeStruct((b, q, c), jnp.bfloat16),
        scratch_shapes=[pltpu.VMEM((HALO + tq, c), jnp.float32)],
        compiler_params=pltpu.CompilerParams(dimension_semantics=("arbitrary",) * 2,
                                             vmem_limit_bytes=VMEM_LIMIT),
        name="pool",
    )(u, u, hist, w_pool_bf, scale)


def _out_proj_kernel(a_ref, p_ref, x_ref, wa_ref, wp_ref, g_ref, wr_hi_ref, wr_lo_ref, br_ref,
                     tri_ref, cin_ref,
                     x1_ref, h_ref, idx_ref, gate_ref, rank_ref, cout_ref, carry_ref):
    i = pl.program_id(0)

    @pl.when(i == 0)
    def _():
        carry_ref[...] = cin_ref[...]

    x1 = x_ref[...] + _dot(a_ref[...], wa_ref[...]) + _dot(p_ref[...], wp_ref[...])
    x1_ref[...] = x1
    ms = jnp.mean(x1 * x1, axis=-1, keepdims=True)
    h = x1 * lax.rsqrt(ms + EPS) * g_ref[...]
    _store_row_tiles(h_ref, h)

    h_hi, h_lo = _split_bf16(h)
    wr_hi = wr_hi_ref[...]
    logits = _dot_nt(wr_hi, h_hi) + _dot_nt(wr_hi, h_lo) + _dot_nt(wr_lo_ref[...], h_hi) + br_ref[...]
    expert = lax.broadcasted_iota(jnp.int32, logits.shape, 0)
    work = logits
    vals, idxs, hots = [], [], []
    for _ in range(TOP_K):
        m = jnp.max(work, axis=0, keepdims=True)
        sel = jnp.min(jnp.where(work == m, expert, N_EXPERTS), axis=0, keepdims=True)
        hot = expert == sel
        work = jnp.where(hot, -jnp.inf, work)
        vals.append(m)
        idxs.append(sel)
        hots.append(hot)
    ex = [jnp.exp(v - vals[0]) for v in vals]
    den = ex[0] + ex[1] + ex[2] + ex[3]
    gate_ref[...] = jnp.concatenate([e / den for e in ex] + [jnp.zeros_like(den)] * (8 - TOP_K), axis=0)
    idx_ref[...] = jnp.concatenate(idxs, axis=1)

    hot_all = hots[0] | hots[1] | hots[2] | hots[3]
    hot_f = jnp.where(hot_all, 1.0, 0.0)
    prefix = _dot(hot_f.astype(jnp.bfloat16), tri_ref[...]) + carry_ref[:, 0:1]
    ranks = [jnp.sum(jnp.where(hot, prefix, 0.0), axis=0, keepdims=True) for hot in hots]
    rank_ref[...] = jnp.concatenate(ranks, axis=1).astype(jnp.int32)
    carry_ref[...] = carry_ref[...] + jnp.sum(hot_f, axis=1, keepdims=True)
    cout_ref[...] = carry_ref[...]


def _out_proj(a, p, x, wa, wp, g, wr_hi, wr_lo, br, tri, carry_in):
    t, d = x.shape
    tm = tri.shape[0]
    row = lambda w: pl.BlockSpec((tm, w), lambda i: (i, 0))
    lst = pl.BlockSpec((None, 1, TOP_K * tm), lambda i: (i, 0, 0))
    full = lambda arr: pl.BlockSpec(arr.shape, lambda i: (0, 0))
    f32, i32 = jnp.float32, jnp.int32
    n_tiles = t // tm
    return pl.pallas_call(
        _out_proj_kernel,
        grid=(n_tiles,),
        in_specs=[row(ATTN_WIDTH), row(POOL_WIDTH), row(d), full(wa), full(wp), full(g),
                  full(wr_hi), full(wr_lo), full(br), full(tri), full(carry_in)],
        out_specs=[row(d), pl.BlockSpec((tm * ROW_SUB, LANES), lambda i: (i, 0)), lst,
                   pl.BlockSpec((8, tm), lambda i: (0, i)), lst, full(carry_in)],
        out_shape=[jax.ShapeDtypeStruct((t, d), f32),
                   jax.ShapeDtypeStruct((t * ROW_SUB, LANES), f32),
                   jax.ShapeDtypeStruct((n_tiles, 1, TOP_K * tm), i32),
                   jax.ShapeDtypeStruct((8, t), f32),
                   jax.ShapeDtypeStruct((n_tiles, 1, TOP_K * tm), i32),
                   jax.ShapeDtypeStruct(carry_in.shape, f32)],
        scratch_shapes=[pltpu.VMEM(carry_in.shape, f32)],
        compiler_params=pltpu.CompilerParams(dimension_semantics=("arbitrary",),
                                             vmem_limit_bytes=VMEM_LIMIT),
        name="out_proj_router",
    )(a, p, x, wa, wp, g, wr_hi, wr_lo, br, tri, carry_in)


def _fetch_slots(slots_hbm, slots_smem, sem, tile):
    return pltpu.make_async_copy(slots_hbm.at[tile, 0], slots_smem, sem)


def _row_tile(ref, row):
    return ref.at[pl.ds(pl.multiple_of(row * ROW_SUB, ROW_SUB), ROW_SUB)]


def _row_tile_span(ref, row, rows):
    return ref.at[pl.ds(pl.multiple_of(row * ROW_SUB, ROW_SUB), rows * ROW_SUB)]


def _dispatch_kernel(pend_ref, padded_ref, slots_hbm, *rest, tm, group_tiles, n_blocks):
    n_groups = len(group_tiles)
    h_refs = rest[:n_groups]
    xs_out, slots_smem, zeros, slot_sem, row_sem = rest[n_groups:]
    i = pl.program_id(0)
    n = pl.num_programs(0)
    zr = zeros.shape[0] // ROW_SUB
    per_block = MOE_TILE // zr

    def zero_block(first_row):
        for c in range(per_block):
            pltpu.make_async_copy(zeros, _row_tile_span(xs_out, first_row + c * zr, zr), row_sem).start()

    def wait_zero_blocks(count):
        def body(_, c):
            pltpu.make_async_copy(zeros, _row_tile_span(xs_out, 0, zr), row_sem).wait()
            return c
        lax.fori_loop(0, count * per_block, body, 0)

    @pl.when(i == 0)
    def _():
        _fetch_slots(slots_hbm, slots_smem, slot_sem, 0).start()
        zeros[...] = jnp.zeros_like(zeros)
        for e in range(N_EXPERTS):
            @pl.when(padded_ref[e] > 0)
            def _():
                zero_block(pend_ref[e] - MOE_TILE)
                wait_zero_blocks(1)
        used = pend_ref[N_EXPERTS - 1] // MOE_TILE

        def spare(b, c):
            zero_block(b * MOE_TILE)
            return c
        lax.fori_loop(used, n_blocks, spare, 0)
        wait_zero_blocks(n_blocks - used)

    _fetch_slots(slots_hbm, slots_smem, slot_sem, i).wait()

    def scatter(h_ref):
        def issue(g, c):
            for rr in range(DMA_UNROLL):
                r = g * DMA_UNROLL + rr
                for k in range(TOP_K):
                    dst = slots_smem[k * tm + r]
                    pltpu.make_async_copy(_row_tile(h_ref, r), _row_tile(xs_out, dst), row_sem).start(priority=k % 2)
            return c

        lax.fori_loop(0, tm // DMA_UNROLL, issue, 0)

        @pl.when(i + 1 < n)
        def _():
            _fetch_slots(slots_hbm, slots_smem, slot_sem, i + 1).start()

        for _ in range(TOP_K):
            pltpu.make_async_copy(h_ref, _row_tile_span(xs_out, 0, tm), row_sem).wait()

    first = 0
    for h_ref, tiles in zip(h_refs, group_tiles):
        pl.when((i >= first) & (i < first + tiles))(functools.partial(scatter, h_ref))
        first += tiles


def _dispatch(pad_end, padded, slots, h_groups, n_pad):
    n_tiles, _, per = slots.shape
    tm = per // TOP_K
    group_tiles = tuple(h.shape[0] // (tm * ROW_SUB) for h in h_groups)
    assert sum(group_tiles) == n_tiles
    kern = functools.partial(_dispatch_kernel, tm=tm, group_tiles=group_tiles, n_blocks=n_pad // MOE_TILE)
    any_spec = pl.BlockSpec(memory_space=pl.ANY)
    in_specs, first = [any_spec], 0
    for tiles in group_tiles:
        in_specs.append(pl.BlockSpec(
            (tm * ROW_SUB, LANES),
            lambda i, pe, pd, first=first, tiles=tiles: (jnp.clip(i - first, 0, tiles - 1), 0)))
        first += tiles
    return pl.pallas_call(
        kern,
        grid_spec=pltpu.PrefetchScalarGridSpec(
            num_scalar_prefetch=2,
            grid=(n_tiles,),
            in_specs=in_specs,
            out_specs=any_spec,
            scratch_shapes=[pltpu.SMEM((per,), jnp.int32),
                            pltpu.VMEM((ZERO_ROWS * ROW_SUB, LANES), jnp.float32),
                            pltpu.SemaphoreType.DMA(()), pltpu.SemaphoreType.DMA(())]),
        out_shape=jax.ShapeDtypeStruct((n_pad * ROW_SUB, LANES), jnp.float32),
        compiler_params=pltpu.CompilerParams(dimension_semantics=("arbitrary",),
                                             vmem_limit_bytes=VMEM_LIMIT),
        name="dispatch",
    )(pad_end, padded, slots, *h_groups)


def _combine_kernel(slots_hbm, os_hbm, x1_ref, gate_ref, y_ref, slots_smem, buf_a, buf_b,
                    slot_sem, sem_a, sem_b, *, tm):
    i = pl.program_id(0)
    n = pl.num_programs(0)

    def gather(buf, sem):
        _fetch_slots(slots_hbm, slots_smem, slot_sem, 0).wait()

        def issue(g, c):
            for rr in range(DMA_UNROLL):
                r = g * DMA_UNROLL + rr
                for k in range(TOP_K):
                    src = slots_smem[k * tm + r]
                    pltpu.make_async_copy(_row_tile(os_hbm, src), _row_tile(buf.at[k], r), sem).start(priority=k % 2)
            return c

        lax.fori_loop(0, tm // DMA_UNROLL, issue, 0)

    def gather_next(buf, sem):
        @pl.when(i + 1 < n)
        def _():
            gather(buf, sem)

            @pl.when(i + 2 < n)
            def _():
                _fetch_slots(slots_hbm, slots_smem, slot_sem, i + 2).start()

    def reduce(buf, sem):
        for k in range(TOP_K):
            pltpu.make_async_copy(_row_tile_span(os_hbm, 0, tm), buf.at[k], sem).wait()
        gate = gate_ref[...].T
        for c in range(ROW_SUB):
            cols = slice(c * LANES, (c + 1) * LANES)
            y = x1_ref[:, cols]
            for k in range(TOP_K):
                y = y + gate[:, k:k + 1] * buf[k, pl.ds(c, tm, stride=ROW_SUB), :]
            y_ref[:, cols] = y

    @pl.when(i == 0)
    def _():
        _fetch_slots(slots_hbm, slots_smem, slot_sem, 0).start()
        gather(buf_a, sem_a)

        @pl.when(n > 1)
        def _():
            _fetch_slots(slots_hbm, slots_smem, slot_sem, 1).start()

    @pl.when(i % 2 == 0)
    def _():
        gather_next(buf_b, sem_b)
        reduce(buf_a, sem_a)

    @pl.when(i % 2 == 1)
    def _():
        gather_next(buf_a, sem_a)
        reduce(buf_b, sem_b)


def _combine(slots, out_sorted, x1, gate_rows):
    n_tiles, _, per = slots.shape
    tm = per // TOP_K
    t, d = x1.shape
    kern = functools.partial(_combine_kernel, tm=tm)
    buf = pltpu.VMEM((TOP_K, tm * ROW_SUB, LANES), jnp.float32)
    dma = pltpu.SemaphoreType.DMA(())
    return pl.pallas_call(
        kern,
        grid=(n_tiles,),
        in_specs=[pl.BlockSpec(memory_space=pl.ANY),
                  pl.BlockSpec(memory_space=pl.ANY),
                  pl.BlockSpec((tm, d), lambda i: (i, 0)),
                  pl.BlockSpec((8, tm), lambda i: (0, i))],
        out_specs=pl.BlockSpec((tm, d), lambda i: (i, 0)),
        out_shape=jax.ShapeDtypeStruct((t, d), jnp.float32),
        scratch_shapes=[pltpu.SMEM((per,), jnp.int32), buf, buf, dma, dma, dma],
        compiler_params=pltpu.CompilerParams(dimension_semantics=("arbitrary",),
                                             vmem_limit_bytes=VMEM_LIMIT),
        name="combine",
    )(slots, out_sorted, x1, gate_rows)


def _expert_kernel(be_ref, nu_ref, x_ref, wg_ref, bg_ref, wu_ref, bu_ref, wd_ref, bd_ref, o_ref,
                   wg_bf, wu_bf, wd_bf):
    i = pl.program_id(0)
    tm = x_ref.shape[0] // ROW_SUB

    @pl.when(i < nu_ref[0])
    def _():
        @pl.when((i == 0) | (be_ref[i] != be_ref[jnp.maximum(i - 1, 0)]))
        def _():
            wg_bf[...] = wg_ref[0].astype(jnp.bfloat16)
            wu_bf[...] = wu_ref[0].astype(jnp.bfloat16)
            wd_bf[...] = wd_ref[0].astype(jnp.bfloat16)

        x = _load_row_tiles(x_ref, tm).astype(jnp.bfloat16)
        g = jnp.minimum(_dot(x, wg_bf[...]) + bg_ref[0], SWIGLU_LIMIT)
        u = jnp.clip(_dot(x, wu_bf[...]) + bu_ref[0], -SWIGLU_LIMIT, SWIGLU_LIMIT)
        a = (u + 1.0) * g * jax.nn.sigmoid(SWIGLU_ALPHA * g)
        _store_row_tiles(o_ref, _dot(a.astype(jnp.bfloat16), wd_bf[...]) + bd_ref[0])

    @pl.when(i >= nu_ref[0])
    def _():
        o_ref[...] = jnp.zeros_like(o_ref)


def _experts(block_e, n_used, x_sorted, wg, bg, wu, bu, wd, bd):
    d, f = wg.shape[1], wg.shape[2]
    blk_rows = MOE_TILE * ROW_SUB
    rows = lambda i, be, nu: (jnp.minimum(i, nu[0] - 1), 0)
    wsel = lambda i, be, nu: (be[i], 0, 0)
    return pl.pallas_call(
        _expert_kernel,
        grid_spec=pltpu.PrefetchScalarGridSpec(
            num_scalar_prefetch=2,
            grid=(x_sorted.shape[0] // blk_rows,),
            in_specs=[pl.BlockSpec((blk_rows, LANES), rows),
                      pl.BlockSpec((1, d, f), wsel), pl.BlockSpec((1, 1, f), wsel),
                      pl.BlockSpec((1, d, f), wsel), pl.BlockSpec((1, 1, f), wsel),
                      pl.BlockSpec((1, f, d), wsel), pl.BlockSpec((1, 1, d), wsel)],
            out_specs=pl.BlockSpec((blk_rows, LANES), lambda i, be, nu: (i, 0)),
            scratch_shapes=[pltpu.VMEM((d, f), jnp.bfloat16),
                            pltpu.VMEM((d, f), jnp.bfloat16),
                            pltpu.VMEM((f, d), jnp.bfloat16)]),
        out_shape=jax.ShapeDtypeStruct(x_sorted.shape, jnp.float32),
        compiler_params=pltpu.CompilerParams(dimension_semantics=("arbitrary",),
                                             vmem_limit_bytes=VMEM_LIMIT),
        name="experts",
    )(block_e, n_used, x_sorted, wg, bg, wu, bu, wd, bd)


def _moe(groups, counts, wg, bg, wu, bu, wd, bd):
    tm = MOE_TILE
    total = sum(g[4].shape[0] for g in groups)
    cnt = counts.astype(jnp.int32)
    padded = (cnt + tm - 1) // tm * tm
    pad_end = jnp.cumsum(padded)
    pad_start = pad_end - padded
    n_blocks = -(-(total * TOP_K) // tm) + N_EXPERTS
    n_pad = n_blocks * tm
    block_start = jnp.arange(n_blocks, dtype=jnp.int32) * tm
    block_e = jnp.minimum(jnp.sum(block_start[:, None] >= pad_end[None, :], axis=1),
                          N_EXPERTS - 1).astype(jnp.int32)
    n_used = (pad_end[-1:] // tm).astype(jnp.int32)
    experts = jnp.arange(N_EXPERTS, dtype=jnp.int32)

    slots = [rank + jnp.sum(jnp.where(idx[..., None] == experts, pad_start, 0), axis=-1)
             for _, idx, _, rank, _ in groups]
    x_sorted = _dispatch(pad_end, padded, jnp.concatenate(slots, axis=0), [g[0] for g in groups], n_pad)
    out_sorted = _experts(block_e, n_used, x_sorted, wg, bg, wu, bu, wd, bd)
    return [_combine(s, out_sorted, x1, gate) for s, (_, _, gate, _, x1) in zip(slots, groups)]


def kernel(x_prompt, x_sample, cache_k, cache_v, state_pool, attn_norm, w_in, q_norm, k_norm,
           lambda_q1, lambda_k1, lambda_q2, lambda_k2, subln, w_pool, pool_scale, w_out,
           ffn_norm, w_router, b_router, w_gate, b_gate, w_up, b_up, w_down, b_down):
    f32, bf16 = jnp.float32, jnp.bfloat16
    bp, sp, d = x_prompt.shape
    bs, ss, _ = x_sample.shape
    assert d == D_MODEL, "row-tiled expert buffers are laid out for D_MODEL-wide token rows"
    depth = w_in.shape[0]
    past = cache_k.shape[2]
    tp, ts = bp * sp, bs * ss
    xp = x_prompt.reshape(tp, d)
    xs = x_sample.reshape(ts, d)

    blk = _pick_tile(sp, ATT_BLOCK, CHUNK)
    tm_p = _pick_tile(tp, ROW_TILE, blk)
    tm_s = _pick_tile(ts, ROW_TILE)
    group = lax.broadcasted_iota(jnp.int32, (QK_WIDTH, QK_WIDTH), 0) // QK_DIM
    seg = (group == group.T).astype(bf16)
    tri = {tm: (lax.broadcasted_iota(jnp.int32, (tm, tm), 0)
                < lax.broadcasted_iota(jnp.int32, (tm, tm), 1)).astype(bf16) for tm in {tm_p, tm_s}}

    kp_l, vp_l, pp_l, ks_l, vs_l, ps_l = [], [], [], [], [], []
    for l in range(depth):
        lam_init = 0.8 - 0.6 * math.exp(-0.3 * l)
        lam_rows = jnp.stack([lambda_q1[l], lambda_k1[l], lambda_q2[l], lambda_k2[l]]).astype(f32)
        g_attn = attn_norm[l].reshape(1, d)
        w_in_bf = w_in[l].astype(bf16)
        wqt = w_in_bf[:, :QK_WIDTH].T
        wvt = w_in_bf[:, 2 * QK_WIDTH:2 * QK_WIDTH + ATTN_WIDTH].T
        qg = jnp.tile(q_norm[l], QK_WIDTH // QK_DIM).astype(f32)
        kg = jnp.tile(k_norm[l], QK_WIDTH // QK_DIM).reshape(1, QK_WIDTH).astype(f32)
        sub = subln[l].astype(f32)
        w_pool_bf = w_pool[l].astype(bf16)
        scale = pool_scale[l].reshape(1, POOL_WIDTH)
        wa = w_out[l][:ATTN_WIDTH].astype(bf16)
        wp = w_out[l][ATTN_WIDTH:].astype(bf16)
        g_ffn = ffn_norm[l].reshape(1, d)
        wr_hi, wr_lo = _split_bf16(w_router[l].T.astype(f32))
        br = b_router[l].reshape(N_EXPERTS, 1).astype(f32)
        bg = b_gate[l].reshape(N_EXPERTS, 1, -1)
        bu = b_up[l].reshape(N_EXPERTS, 1, -1)
        bd = b_down[l].reshape(N_EXPERTS, 1, -1)

        qt, kb, vt, k, v, u = _in_proj(xp, g_attn, w_in_bf, wqt, wvt, qg.reshape(QK_WIDTH, 1), kg, seg, True, blk)
        a = _prompt_attention(qt, kb, vt, lam_rows, sub.reshape(V_DIM, 1), bp, sp, blk, lam_init)
        u3 = u.reshape(bp, sp, POOL_WIDTH)
        o_pool = _pool(u3, jnp.zeros((bp, HALO, POOL_WIDTH), f32), w_pool_bf, scale, 0)
        carry0 = jnp.zeros((N_EXPERTS, 128), f32)
        x1p, hp, idx_p, gate_p, rank_p, carry1 = _out_proj(
            a, o_pool.reshape(tp, POOL_WIDTH), xp, wa, wp, g_ffn, wr_hi, wr_lo, br, tri[tm_p], carry0)
        kp_l.append(k.reshape(bp, sp, N_HEADS, 2, QK_DIM))
        vp_l.append(v.reshape(bp, sp, N_HEADS, V_DIM))
        pp_l.append(u3[:, sp - POOL_STATE:])

        q, kb, vb, k, v, u = _in_proj(xs, g_attn, w_in_bf, wqt, wvt, qg.reshape(1, QK_WIDTH), kg, seg, False, blk)
        a = _sample_attention(q, kb, vb, cache_k[l].reshape(bs, past, QK_WIDTH),
                              cache_v[l].reshape(bs, past, ATTN_WIDTH), lam_rows, sub.reshape(1, V_DIM),
                              bs, ss, lam_init)
        u3 = u.reshape(bs, ss, POOL_WIDTH)
        hist = jnp.concatenate([jnp.zeros((bs, 1, POOL_WIDTH), f32), state_pool[l]], axis=1)
        o_pool = _pool(u3, hist, w_pool_bf, scale, past)
        x1s, hs, idx_s, gate_s, rank_s, carry2 = _out_proj(
            a, o_pool.reshape(ts, POOL_WIDTH), xs, wa, wp, g_ffn, wr_hi, wr_lo, br, tri[tm_s], carry1)
        ks_l.append(k.reshape(bs, ss, N_HEADS, 2, QK_DIM))
        vs_l.append(v.reshape(bs, ss, N_HEADS, V_DIM))
        ps_l.append(jnp.concatenate([state_pool[l], u3], axis=1)[:, -POOL_STATE:])

        xp, xs = _moe([(hp, idx_p, gate_p, rank_p, x1p), (hs, idx_s, gate_s, rank_s, x1s)],
                      carry2[:, 0], w_gate[l], bg, w_up[l], bu, w_down[l], bd)

    return (xp.reshape(bp, sp, d), xs.reshape(bs, ss, d), jnp.stack(kp_l), jnp.stack(vp_l),
            jnp.stack(pp_l), jnp.stack(ks_l), jnp.stack(vs_l), jnp.stack(ps_l))
```

```python
import functools
import math

import jax
import jax.numpy as jnp
from jax import lax
from jax.experimental import pallas as pl
from jax.experimental.pallas import tpu as pltpu

CHUNK = 64
N_HEADS = 4
QK_DIM = 64
V_DIM = 128
HEAD_W = 2 * QK_DIM
QK_WIDTH = N_HEADS * HEAD_W
ATTN_WIDTH = N_HEADS * V_DIM
POOL_WINDOWS = (2, 4, 8, 16)
POOL_GROUP = 128
POOL_WIDTH = POOL_GROUP * len(POOL_WINDOWS)
POOL_STATE = max(POOL_WINDOWS) - 1
HALO = POOL_STATE + 1
N_EXPERTS = 32
TOP_K = 4
SWIGLU_LIMIT = 7.0
SWIGLU_ALPHA = 1.702
EPS = 1e-6
LOG2E = 1.4426950408889634
D_MODEL = 1024
LANES = 128
ROW_SUB = D_MODEL // LANES

ROW_TILE = 512
ATT_BLOCK = 256
MOE_TILE = 512
DMA_UNROLL = 8
ZERO_ROWS = 64
MAX_FIXED_SHIFT = 60.0
VMEM_LIMIT = 56 * 1024 * 1024


def _pick_tile(n, cap, mult=8):
    t = min(n, cap)
    while n % t or t % mult:
        t -= 1
    return t


def _split_bf16(x):
    hi = x.astype(jnp.bfloat16)
    lo = (x - hi.astype(jnp.float32)).astype(jnp.bfloat16)
    return hi, lo


def _dot_nt(a, b):
    return lax.dot_general(a, b, (((1,), (1,)), ((), ())), preferred_element_type=jnp.float32)


def _dot(a, b):
    return jnp.dot(a, b, preferred_element_type=jnp.float32)


def _store_row_tiles(ref, x):
    m = x.shape[0]
    for c in range(ROW_SUB):
        ref[pl.ds(c, m, stride=ROW_SUB), :] = x[:, c * LANES:(c + 1) * LANES]


def _load_row_tiles(ref, m):
    return jnp.concatenate([ref[pl.ds(c, m, stride=ROW_SUB), :] for c in range(ROW_SUB)], axis=1)


def _normed_bf16(x_ref, g_ref):
    x = x_ref[...]
    ms = jnp.mean(x * x, axis=-1, keepdims=True)
    return (x * lax.rsqrt(ms + EPS) * g_ref[...]).astype(jnp.bfloat16)


def _in_proj_prompt_kernel(x_ref, g_ref, wvu_ref, wqkt_ref, gain_ref,
                           q_ref, kb_ref, vt_ref, k_ref, v_ref, u_ref, *, blk):
    hn = _normed_bf16(x_ref, g_ref)
    tm = hn.shape[0]
    z = _dot(hn, wvu_ref[...])
    v = z[:, :ATTN_WIDTH]
    for c in range(ATTN_WIDTH // LANES):
        v_ref[pl.ds(c, tm, stride=ATTN_WIDTH // LANES), :] = v[:, c * LANES:(c + 1) * LANES]
    u_ref[...] = z[:, ATTN_WIDTH:]

    t = _dot_nt(wqkt_ref[...], hn)
    t3 = t.reshape(2 * QK_WIDTH // QK_DIM, QK_DIM, tm)
    r = lax.rsqrt(jnp.mean(t3 * t3, axis=1, keepdims=True) + EPS)
    tn = (t3 * r).reshape(2 * QK_WIDTH, tm) * gain_ref[...]
    qn, kn = tn[:QK_WIDTH], tn[QK_WIDTH:]
    k_ref[...] = kn
    kb_ref[...] = kn.T.astype(jnp.bfloat16)
    vt = v.T
    for s in range(tm // blk):
        q_ref[s] = qn[:, s * blk:(s + 1) * blk].astype(jnp.bfloat16)
        vt_ref[s] = vt[:, s * blk:(s + 1) * blk].astype(jnp.bfloat16)


def _in_proj_prompt(x, g, wvu, wqkt, gain, batch, seq, blk):
    t, d = x.shape
    tm = _pick_tile(seq, ROW_TILE, blk)
    per_seq = seq // tm
    row = lambda w: pl.BlockSpec((tm, w), lambda i: (i, 0))
    full = lambda a: pl.BlockSpec(a.shape, lambda i: (0, 0))
    slab = pl.BlockSpec((tm // blk, QK_WIDTH, blk), lambda i: (i, 0, 0))
    f32, bf16 = jnp.float32, jnp.bfloat16
    kern = functools.partial(_in_proj_prompt_kernel, blk=blk)
    return pl.pallas_call(
        kern,
        grid=(t // tm,),
        in_specs=[row(d), full(g), full(wvu), full(wqkt), full(gain)],
        out_specs=[slab, row(QK_WIDTH), slab,
                   pl.BlockSpec((None, QK_WIDTH, tm), lambda i: (i // per_seq, 0, i % per_seq)),
                   pl.BlockSpec((tm * N_HEADS, V_DIM), lambda i: (i, 0)),
                   row(POOL_WIDTH)],
        out_shape=[jax.ShapeDtypeStruct((t // blk, QK_WIDTH, blk), bf16),
                   jax.ShapeDtypeStruct((t, QK_WIDTH), bf16),
                   jax.ShapeDtypeStruct((t // blk, ATTN_WIDTH, blk), bf16),
                   jax.ShapeDtypeStruct((batch, QK_WIDTH, seq), f32),
                   jax.ShapeDtypeStruct((t * N_HEADS, V_DIM), f32),
                   jax.ShapeDtypeStruct((t, POOL_WIDTH), f32)],
        compiler_params=pltpu.CompilerParams(dimension_semantics=("arbitrary",),
                                             vmem_limit_bytes=VMEM_LIMIT),
        name="in_proj_prompt",
    )(x, g, wvu, wqkt, gain)


def _in_proj_sample_kernel(x_ref, g_ref, w_ref, qg_ref, kg_ref, seg_ref,
                           q_ref, kb_ref, vb_ref, k_ref, v_ref, u_ref):
    hn = _normed_bf16(x_ref, g_ref)
    z = _dot(hn, w_ref[...])
    seg = seg_ref[...]

    def qk_norm(t, gain):
        hi, lo = _split_bf16(t * t)
        ss = _dot(hi, seg) + _dot(lo, seg)
        return t * lax.rsqrt(ss * (1.0 / QK_DIM) + EPS) * gain

    qn = qk_norm(z[:, :QK_WIDTH], qg_ref[...])
    kn = qk_norm(z[:, QK_WIDTH:2 * QK_WIDTH], kg_ref[...])
    v = z[:, 2 * QK_WIDTH:2 * QK_WIDTH + ATTN_WIDTH]
    q_ref[...] = (qn * (QK_DIM ** -0.5)).astype(jnp.bfloat16)
    k_ref[...] = kn
    kb_ref[...] = kn.astype(jnp.bfloat16)
    v_ref[...] = v
    vb_ref[...] = v.astype(jnp.bfloat16)
    u_ref[...] = z[:, 2 * QK_WIDTH + ATTN_WIDTH:]


def _in_proj_sample(x, g, w_bf, qg, kg, seg):
    t, d = x.shape
    tm = _pick_tile(t, ROW_TILE)
    row = lambda w: pl.BlockSpec((tm, w), lambda i: (i, 0))
    full = lambda a: pl.BlockSpec(a.shape, lambda i: (0, 0))
    f32, bf16 = jnp.float32, jnp.bfloat16
    return pl.pallas_call(
        _in_proj_sample_kernel,
        grid=(t // tm,),
        in_specs=[row(d), full(g), full(w_bf), full(qg), full(kg), full(seg)],
        out_specs=[row(QK_WIDTH)] * 2 + [row(ATTN_WIDTH), row(QK_WIDTH), row(ATTN_WIDTH), row(POOL_WIDTH)],
        out_shape=[jax.ShapeDtypeStruct((t, QK_WIDTH), bf16),
                   jax.ShapeDtypeStruct((t, QK_WIDTH), bf16),
                   jax.ShapeDtypeStruct((t, ATTN_WIDTH), bf16),
                   jax.ShapeDtypeStruct((t, QK_WIDTH), f32),
                   jax.ShapeDtypeStruct((t, ATTN_WIDTH), f32),
                   jax.ShapeDtypeStruct((t, POOL_WIDTH), f32)],
        compiler_params=pltpu.CompilerParams(dimension_semantics=("arbitrary",),
                                             vmem_limit_bytes=VMEM_LIMIT),
        name="in_proj_sample",
    )(x, g, w_bf, qg, kg, seg)


def _lambda_scalar(lam_ref, lam_init):
    l = lam_ref[...]
    a = jnp.sum(l[0:1] * l[1:2], axis=-1, keepdims=True)
    b = jnp.sum(l[2:3] * l[3:4], axis=-1, keepdims=True)
    return jnp.exp(a) - jnp.exp(b) + lam_init


def _split_maps(q, axis):
    pos = lax.broadcasted_iota(jnp.int32, q.shape, axis)
    zero = jnp.zeros_like(q)
    return jnp.where(pos < QK_DIM, q, zero), jnp.where(pos >= QK_DIM, q, zero)


def _prompt_attn_kernel(bound_ref, lam_ref, subln_ref, q_ref, k_ref, v_ref, o_ref,
                        s_even, s_odd, p_even, p_odd, *, lam_init, blk):
    i = pl.program_id(2)
    bound = bound_ref[0, 0]
    lam = _lambda_scalar(lam_ref, lam_init)
    q1, q2 = _split_maps(q_ref[...], 0)
    qs = (q1, q2)

    def scores(j, s_buf):
        kb = k_ref[pl.ds(pl.multiple_of(j * blk, blk), blk), :]
        for c in range(2):
            s_buf[c] = _dot(kb, qs[c])

    def stage(carry, s_cur, p_prev, v_prev, p_cur, masked, nxt, fixed):
        if nxt is not None:
            scores(*nxt)
        vb = v_ref[v_prev]
        out = []
        for c in range(2):
            m, l, acc = carry[c]
            pv = _dot(vb, p_prev[c])
            s = s_cur[c]
            if masked:
                kc = lax.broadcasted_iota(jnp.int32, s.shape, 0) // CHUNK
                qc = lax.broadcasted_iota(jnp.int32, s.shape, 1) // CHUNK
                s = jnp.where(kc <= qc, s, -jnp.inf)
            if fixed:
                p = jnp.exp2(s - bound)
                out.append((m, l + jnp.sum(p, axis=0, keepdims=True), acc + pv))
            else:
                m_new = jnp.maximum(m, jnp.max(s, axis=0, keepdims=True))
                alpha = jnp.exp2(m - m_new)
                p = jnp.exp2(s - m_new)
                out.append((m_new, alpha * l + jnp.sum(p, axis=0, keepdims=True), (acc + pv) * alpha))
            p_cur[c] = p.astype(jnp.bfloat16)
        return tuple(out)

    def finish(carry, p_last):
        vb = v_ref[i]
        o = []
        for c in range(2):
            _, l, acc = carry[c]
            o.append((acc + _dot(vb, p_last[c])) / l)
        o = o[0] - lam * o[1]
        ms = jnp.mean(o * o, axis=0, keepdims=True)
        o = o * lax.rsqrt(ms + EPS) * (subln_ref[...] * (1.0 - lam_init))
        o_ref[...] = o.T.astype(o_ref.dtype)

    def run(fixed):
        p_odd[...] = jnp.zeros_like(p_odd)
        scores(0, s_even)
        neg = jnp.full((1, blk), -jnp.inf, jnp.float32)
        zl = jnp.zeros((1, blk), jnp.float32)
        za = jnp.zeros((V_DIM, blk), jnp.float32)

        def pair(jj, carry):
            j = 2 * jj
            carry = stage(carry, s_even, p_odd, jnp.maximum(j - 1, 0), p_even, False, (j + 1, s_odd), fixed)
            return stage(carry, s_odd, p_even, j, p_odd, False, (j + 2, s_even), fixed)

        carry = lax.fori_loop(0, i // 2, pair, ((neg, zl, za), (neg, zl, za)))

        @pl.when(i % 2 == 0)
        def _():
            finish(stage(carry, s_even, p_odd, jnp.maximum(i - 1, 0), p_even, True, None, fixed), p_even)

        @pl.when(i % 2 == 1)
        def _():
            c = stage(carry, s_even, p_odd, jnp.maximum(i - 2, 0), p_even, False, (i, s_odd), fixed)
            finish(stage(c, s_odd, p_even, i - 1, p_odd, True, None, fixed), p_odd)

    small = bound <= MAX_FIXED_SHIFT
    pl.when(small)(functools.partial(run, True))
    pl.when(jnp.logical_not(small))(functools.partial(run, False))


def _prompt_attention(bound, qt, k, vt, lam_rows, subln_col, batch, seq, blk, lam_init):
    nq = seq // blk
    t = batch * seq
    kern = functools.partial(_prompt_attn_kernel, lam_init=lam_init, blk=blk)
    return pl.pallas_call(
        kern,
        grid=(batch, N_HEADS, nq),
        in_specs=[pl.BlockSpec(memory_space=pltpu.SMEM),
                  pl.BlockSpec(lam_rows.shape, lambda b, h, i: (0, 0)),
                  pl.BlockSpec(subln_col.shape, lambda b, h, i: (0, 0)),
                  pl.BlockSpec((None, HEAD_W, blk), lambda b, h, i: (b * nq + i, h, 0)),
                  pl.BlockSpec((seq, HEAD_W), lambda b, h, i: (b, h)),
                  pl.BlockSpec((nq, V_DIM, blk), lambda b, h, i: (b, h, 0))],
        out_specs=pl.BlockSpec((blk, V_DIM), lambda b, h, i: (b * nq + i, h)),
        out_shape=jax.ShapeDtypeStruct((t, ATTN_WIDTH), jnp.bfloat16),
        scratch_shapes=[pltpu.VMEM((2, blk, blk), jnp.float32), pltpu.VMEM((2, blk, blk), jnp.float32),
                        pltpu.VMEM((2, blk, blk), jnp.bfloat16), pltpu.VMEM((2, blk, blk), jnp.bfloat16)],
        compiler_params=pltpu.CompilerParams(dimension_semantics=("arbitrary",) * 3,
                                             vmem_limit_bytes=VMEM_LIMIT),
        name="prompt_attention",
    )(bound, lam_rows, subln_col, qt, k, vt)


def _sample_attn_kernel(lam_ref, subln_ref, q_ref, kn_ref, vn_ref, ck_ref, cv_ref, o_ref, *, lam_init, past):
    lam = _lambda_scalar(lam_ref, lam_init)
    q = q_ref[...]
    nq = q.shape[0]
    qpos = past + lax.broadcasted_iota(jnp.int32, (nq, nq), 0)
    kpos = past + lax.broadcasted_iota(jnp.int32, (nq, nq), 1)
    vis_new = (kpos // CHUNK) <= (qpos // CHUNK)
    for h in range(N_HEADS):
        qk = slice(h * HEAD_W, (h + 1) * HEAD_W)
        vv = slice(h * V_DIM, (h + 1) * V_DIM)
        q1, q2 = _split_maps(q[:, qk], 1)
        kc = ck_ref[0, qk, :].astype(jnp.bfloat16)
        vc = cv_ref[0, pl.ds(h, past, stride=N_HEADS), :].astype(jnp.bfloat16)
        kn = kn_ref[:, qk]
        vn = vn_ref[:, vv]

        def one_map(qm):
            sc = _dot(qm, kc)
            sn = jnp.where(vis_new, _dot_nt(qm, kn), -jnp.inf)
            m = jnp.maximum(jnp.max(sc, axis=-1, keepdims=True), jnp.max(sn, axis=-1, keepdims=True))
            pc = jnp.exp(sc - m)
            pn = jnp.exp(sn - m)
            l = jnp.sum(pc, axis=-1, keepdims=True) + jnp.sum(pn, axis=-1, keepdims=True)
            return (_dot(pc.astype(jnp.bfloat16), vc) + _dot(pn.astype(jnp.bfloat16), vn)) / l

        o = one_map(q1) - lam * one_map(q2)
        ms = jnp.mean(o * o, axis=-1, keepdims=True)
        o_ref[:, vv] = (o * lax.rsqrt(ms + EPS) * subln_ref[...] * (1.0 - lam_init)).astype(o_ref.dtype)


def _sample_attention(q, k_new, v_new, cache_k, cache_v, lam_rows, subln, batch, nq, lam_init):
    past = cache_k.shape[2]
    kern = functools.partial(_sample_attn_kernel, lam_init=lam_init, past=past)
    row = lambda w: pl.BlockSpec((nq, w), lambda b: (b, 0))
    return pl.pallas_call(
        kern,
        grid=(batch,),
        in_specs=[pl.BlockSpec(lam_rows.shape, lambda b: (0, 0)),
                  pl.BlockSpec(subln.shape, lambda b: (0, 0)),
                  row(QK_WIDTH), row(QK_WIDTH), row(ATTN_WIDTH),
                  pl.BlockSpec((1, QK_WIDTH, past), lambda b: (b, 0, 0)),
                  pl.BlockSpec((1, past * N_HEADS, V_DIM), lambda b: (b, 0, 0))],
        out_specs=row(ATTN_WIDTH),
        out_shape=jax.ShapeDtypeStruct((batch * nq, ATTN_WIDTH), jnp.bfloat16),
        compiler_params=pltpu.CompilerParams(dimension_semantics=("arbitrary",),
                                             vmem_limit_bytes=VMEM_LIMIT),
        name="sample_attention",
    )(lam_rows, subln, q, k_new, v_new, cache_k, cache_v)


def _pool_kernel(u_ref, prev_ref, hist_ref, w_ref, scale_ref, o_ref, ext_ref, *, pos0, tq):
    j = pl.program_id(1)
    ext_ref[0:HALO, :] = jnp.where(j == 0, hist_ref[0], prev_ref[0])
    ext_ref[HALO:, :] = u_ref[0]
    pos = pos0 + j * tq + lax.broadcasted_iota(jnp.int32, (tq, 1), 0)
    for g, w in enumerate(POOL_WINDOWS):
        cols = slice(g * POOL_GROUP, (g + 1) * POOL_GROUP)
        acc = ext_ref[HALO:, cols]
        for back in range(1, w):
            acc = acc + ext_ref[HALO - back:HALO - back + tq, cols]
        count = jnp.minimum(pos + 1, w).astype(jnp.float32)
        d = acc / count - ext_ref[HALO:, cols]
        y = _dot(d.astype(jnp.bfloat16), w_ref[g])
        o_ref[0, :, cols] = (y * scale_ref[:, cols]).astype(o_ref.dtype)


def _pool(u, hist, w_pool_bf, scale, pos0):
    b, q, c = u.shape
    tq = _pick_tile(q, ROW_TILE, HALO)
    per = tq // HALO
    kern = functools.partial(_pool_kernel, pos0=pos0, tq=tq)
    return pl.pallas_call(
        kern,
        grid=(b, q // tq),
        in_specs=[pl.BlockSpec((1, tq, c), lambda i, j: (i, j, 0)),
                  pl.BlockSpec((1, HALO, c), lambda i, j: (i, jnp.maximum(j * per - 1, 0), 0)),
                  pl.BlockSpec((1, HALO, c), lambda i, j: (i, 0, 0)),
                  pl.BlockSpec(w_pool_bf.shape, lambda i, j: (0, 0, 0)),
                  pl.BlockSpec(scale.shape, lambda i, j: (0, 0))],
        out_specs=pl.BlockSpec((1, tq, c), lambda i, j: (i, j, 0)),
        out_shape=jax.ShapeDtypeStruct((b, q, c), jnp.bfloat16),
        scratch_shapes=[pltpu.VMEM((HALO + tq, c), jnp.float32)],
        compiler_params=pltpu.CompilerParams(dimension_semantics=("arbitrary",) * 2,
                                             vmem_limit_bytes=VMEM_LIMIT),
        name="pool",
    )(u, u, hist, w_pool_bf, scale)


def _out_proj_kernel(a_ref, p_ref, x_ref, wa_ref, wp_ref, g_ref, wr_hi_ref, wr_lo_ref, br_ref,
                     tri_ref, cin_ref,
                     x1_ref, h_ref, idx_ref, gate_ref, rank_ref, cout_ref, carry_ref):
    i = pl.program_id(0)

    @pl.when(i == 0)
    def _():
        carry_ref[...] = cin_ref[...]

    x1 = x_ref[...] + _dot(a_ref[...], wa_ref[...]) + _dot(p_ref[...], wp_ref[...])
    x1_ref[...] = x1
    ms = jnp.mean(x1 * x1, axis=-1, keepdims=True)
    h = x1 * lax.rsqrt(ms + EPS) * g_ref[...]
    _store_row_tiles(h_ref, h)

    h_hi, h_lo = _split_bf16(h)
    wr_hi = wr_hi_ref[...]
    logits = _dot_nt(wr_hi, h_hi) + _dot_nt(wr_hi, h_lo) + _dot_nt(wr_lo_ref[...], h_hi) + br_ref[...]
    expert = lax.broadcasted_iota(jnp.int32, logits.shape, 0)
    work = logits
    vals, idxs, hots = [], [], []
    for _ in range(TOP_K):
        m = jnp.max(work, axis=0, keepdims=True)
        sel = jnp.min(jnp.where(work == m, expert, N_EXPERTS), axis=0, keepdims=True)
        hot = expert == sel
        work = jnp.where(hot, -jnp.inf, work)
        vals.append(m)
        idxs.append(sel)
        hots.append(hot)
    ex = [jnp.exp(v - vals[0]) for v in vals]
    den = ex[0] + ex[1] + ex[2] + ex[3]
    gate_ref[...] = jnp.concatenate([e / den for e in ex] + [jnp.zeros_like(den)] * (8 - TOP_K), axis=0)
    idx_ref[...] = jnp.concatenate(idxs, axis=1)

    hot_all = hots[0] | hots[1] | hots[2] | hots[3]
    hot_f = jnp.where(hot_all, 1.0, 0.0)
    prefix = _dot(hot_f.astype(jnp.bfloat16), tri_ref[...]) + carry_ref[:, 0:1]
    ranks = [jnp.sum(jnp.where(hot, prefix, 0.0), axis=0, keepdims=True) for hot in hots]
    rank_ref[...] = jnp.concatenate(ranks, axis=1).astype(jnp.int32)
    carry_ref[...] = carry_ref[...] + jnp.sum(hot_f, axis=1, keepdims=True)
    cout_ref[...] = carry_ref[...]


def _out_proj(a, p, x, wa, wp, g, wr_hi, wr_lo, br, tri, carry_in):
    t, d = x.shape
    tm = tri.shape[0]
    row = lambda w: pl.BlockSpec((tm, w), lambda i: (i, 0))
    lst = pl.BlockSpec((None, 1, TOP_K * tm), lambda i: (i, 0, 0))
    full = lambda arr: pl.BlockSpec(arr.shape, lambda i: (0, 0))
    f32, i32 = jnp.float32, jnp.int32
    n_tiles = t // tm
    return pl.pallas_call(
        _out_proj_kernel,
        grid=(n_tiles,),
        in_specs=[row(ATTN_WIDTH), row(POOL_WIDTH), row(d), full(wa), full(wp), full(g),
                  full(wr_hi), full(wr_lo), full(br), full(tri), full(carry_in)],
        out_specs=[row(d), pl.BlockSpec((tm * ROW_SUB, LANES), lambda i: (i, 0)), lst,
                   pl.BlockSpec((8, tm), lambda i: (0, i)), lst, full(carry_in)],
        out_shape=[jax.ShapeDtypeStruct((t, d), f32),
                   jax.ShapeDtypeStruct((t * ROW_SUB, LANES), f32),
                   jax.ShapeDtypeStruct((n_tiles, 1, TOP_K * tm), i32),
                   jax.ShapeDtypeStruct((8, t), f32),
                   jax.ShapeDtypeStruct((n_tiles, 1, TOP_K * tm), i32),
                   jax.ShapeDtypeStruct(carry_in.shape, f32)],
        scratch_shapes=[pltpu.VMEM(carry_in.shape, f32)],
        compiler_params=pltpu.CompilerParams(dimension_semantics=("arbitrary",),
                                             vmem_limit_bytes=VMEM_LIMIT),
        name="out_proj_router",
    )(a, p, x, wa, wp, g, wr_hi, wr_lo, br, tri, carry_in)


def _fetch_slots(slots_hbm, slots_smem, sem, tile):
    return pltpu.make_async_copy(slots_hbm.at[tile, 0], slots_smem, sem)


def _row_tile(ref, row):
    return ref.at[pl.ds(pl.multiple_of(row * ROW_SUB, ROW_SUB), ROW_SUB)]


def _row_tile_span(ref, row, rows):
    return ref.at[pl.ds(pl.multiple_of(row * ROW_SUB, ROW_SUB), rows * ROW_SUB)]


def _dispatch_kernel(pend_ref, padded_ref, slots_hbm, *rest, tm, group_tiles, n_blocks):
    n_groups = len(group_tiles)
    h_refs = rest[:n_groups]
    xs_out, slots_smem, zeros, slot_sem, row_sem = rest[n_groups:]
    i = pl.program_id(0)
    n = pl.num_programs(0)
    zr = zeros.shape[0] // ROW_SUB
    per_block = MOE_TILE // zr

    def zero_block(first_row):
        for c in range(per_block):
            pltpu.make_async_copy(zeros, _row_tile_span(xs_out, first_row + c * zr, zr), row_sem).start()

    def wait_zero_blocks(count):
        def body(_, c):
            pltpu.make_async_copy(zeros, _row_tile_span(xs_out, 0, zr), row_sem).wait()
            return c
        lax.fori_loop(0, count * per_block, body, 0)

    @pl.when(i == 0)
    def _():
        _fetch_slots(slots_hbm, slots_smem, slot_sem, 0).start()
        zeros[...] = jnp.zeros_like(zeros)
        for e in range(N_EXPERTS):
            @pl.when(padded_ref[e] > 0)
            def _():
                zero_block(pend_ref[e] - MOE_TILE)
                wait_zero_blocks(1)
        used = pend_ref[N_EXPERTS - 1] // MOE_TILE

        def spare(b, c):
            zero_block(b * MOE_TILE)
            return c
        lax.fori_loop(used, n_blocks, spare, 0)
        wait_zero_blocks(n_blocks - used)

    _fetch_slots(slots_hbm, slots_smem, slot_sem, i).wait()

    def scatter(h_ref):
        def issue(g, c):
            for rr in range(DMA_UNROLL):
                r = g * DMA_UNROLL + rr
                for k in range(TOP_K):
                    dst = slots_smem[k * tm + r]
                    pltpu.make_async_copy(_row_tile(h_ref, r), _row_tile(xs_out, dst), row_sem).start(priority=k % 2)
            return c

        lax.fori_loop(0, tm // DMA_UNROLL, issue, 0)

        @pl.when(i + 1 < n)
        def _():
            _fetch_slots(slots_hbm, slots_smem, slot_sem, i + 1).start()

        for _ in range(TOP_K):
            pltpu.make_async_copy(h_ref, _row_tile_span(xs_out, 0, tm), row_sem).wait()

    first = 0
    for h_ref, tiles in zip(h_refs, group_tiles):
        pl.when((i >= first) & (i < first + tiles))(functools.partial(scatter, h_ref))
        first += tiles


def _dispatch(pad_end, padded, slots, h_groups, n_pad):
    n_tiles, _, per = slots.shape
    tm = per // TOP_K
    group_tiles = tuple(h.shape[0] // (tm * ROW_SUB) for h in h_groups)
    assert sum(group_tiles) == n_tiles
    kern = functools.partial(_dispatch_kernel, tm=tm, group_tiles=group_tiles, n_blocks=n_pad // MOE_TILE)
    any_spec = pl.BlockSpec(memory_space=pl.ANY)
    in_specs, first = [any_spec], 0
    for tiles in group_tiles:
        in_specs.append(pl.BlockSpec(
            (tm * ROW_SUB, LANES),
            lambda i, pe, pd, first=first, tiles=tiles: (jnp.clip(i - first, 0, tiles - 1), 0)))
        first += tiles
    return pl.pallas_call(
        kern,
        grid_spec=pltpu.PrefetchScalarGridSpec(
            num_scalar_prefetch=2,
            grid=(n_tiles,),
            in_specs=in_specs,
            out_specs=any_spec,
            scratch_shapes=[pltpu.SMEM((per,), jnp.int32),
                            pltpu.VMEM((ZERO_ROWS * ROW_SUB, LANES), jnp.float32),
                            pltpu.SemaphoreType.DMA(()), pltpu.SemaphoreType.DMA(())]),
        out_shape=jax.ShapeDtypeStruct((n_pad * ROW_SUB, LANES), jnp.float32),
        compiler_params=pltpu.CompilerParams(dimension_semantics=("arbitrary",),
                                             vmem_limit_bytes=VMEM_LIMIT),
        name="dispatch",
    )(pad_end, padded, slots, *h_groups)


def _combine_kernel(slots_hbm, os_hbm, x1_ref, gate_ref, y_ref, slots_smem, buf_a, buf_b,
                    slot_sem, sem_a, sem_b, *, tm):
    i = pl.program_id(0)
    n = pl.num_programs(0)

    def gather(buf, sem):
        _fetch_slots(slots_hbm, slots_smem, slot_sem, 0).wait()

        def issue(g, c):
            for rr in range(DMA_UNROLL):
                r = g * DMA_UNROLL + rr
                for k in range(TOP_K):
                    src = slots_smem[k * tm + r]
                    pltpu.make_async_copy(_row_tile(os_hbm, src), _row_tile(buf.at[k], r), sem).start(priority=k % 2)
            return c

        lax.fori_loop(0, tm // DMA_UNROLL, issue, 0)

    def gather_next(buf, sem):
        @pl.when(i + 1 < n)
        def _():
            gather(buf, sem)

            @pl.when(i + 2 < n)
            def _():
                _fetch_slots(slots_hbm, slots_smem, slot_sem, i + 2).start()

    def reduce(buf, sem):
        for k in range(TOP_K):
            pltpu.make_async_copy(_row_tile_span(os_hbm, 0, tm), buf.at[k], sem).wait()
        gate = gate_ref[...].T
        for c in range(ROW_SUB):
            cols = slice(c * LANES, (c + 1) * LANES)
            y = x1_ref[:, cols]
            for k in range(TOP_K):
                y = y + gate[:, k:k + 1] * buf[k, pl.ds(c, tm, stride=ROW_SUB), :]
            y_ref[:, cols] = y

    @pl.when(i == 0)
    def _():
        _fetch_slots(slots_hbm, slots_smem, slot_sem, 0).start()
        gather(buf_a, sem_a)

        @pl.when(n > 1)
        def _():
            _fetch_slots(slots_hbm, slots_smem, slot_sem, 1).start()

    @pl.when(i % 2 == 0)
    def _():
        gather_next(buf_b, sem_b)
        reduce(buf_a, sem_a)

    @pl.when(i % 2 == 1)
    def _():
        gather_next(buf_a, sem_a)
        reduce(buf_b, sem_b)


def _combine(slots, out_sorted, x1, gate_rows):
    n_tiles, _, per = slots.shape
    tm = per // TOP_K
    t, d = x1.shape
    kern = functools.partial(_combine_kernel, tm=tm)
    buf = pltpu.VMEM((TOP_K, tm * ROW_SUB, LANES), jnp.float32)
    dma = pltpu.SemaphoreType.DMA(())
    return pl.pallas_call(
        kern,
        grid=(n_tiles,),
        in_specs=[pl.BlockSpec(memory_space=pl.ANY),
                  pl.BlockSpec(memory_space=pl.ANY),
                  pl.BlockSpec((tm, d), lambda i: (i, 0)),
                  pl.BlockSpec((8, tm), lambda i: (0, i))],
        out_specs=pl.BlockSpec((tm, d), lambda i: (i, 0)),
        out_shape=jax.ShapeDtypeStruct((t, d), jnp.float32),
        scratch_shapes=[pltpu.SMEM((per,), jnp.int32), buf, buf, dma, dma, dma],
        compiler_params=pltpu.CompilerParams(dimension_semantics=("arbitrary",),
                                             vmem_limit_bytes=VMEM_LIMIT),
        name="combine",
    )(slots, out_sorted, x1, gate_rows)


def _expert_kernel(be_ref, nu_ref, x_ref, wg_ref, bg_ref, wu_ref, bu_ref, wd_ref, bd_ref, o_ref,
                   wg_bf, wu_bf, wd_bf):
    i = pl.program_id(0)
    tm = x_ref.shape[0] // ROW_SUB

    @pl.when(i < nu_ref[0])
    def _():
        @pl.when((i == 0) | (be_ref[i] != be_ref[jnp.maximum(i - 1, 0)]))
        def _():
            wg_bf[...] = wg_ref[0].astype(jnp.bfloat16)
            wu_bf[...] = wu_ref[0].astype(jnp.bfloat16)
            wd_bf[...] = wd_ref[0].astype(jnp.bfloat16)

        x = _load_row_tiles(x_ref, tm).astype(jnp.bfloat16)
        g = jnp.minimum(_dot(x, wg_bf[...]) + bg_ref[0], SWIGLU_LIMIT)
        u = jnp.clip(_dot(x, wu_bf[...]) + bu_ref[0], -SWIGLU_LIMIT, SWIGLU_LIMIT)
        a = (u + 1.0) * g * jax.nn.sigmoid(SWIGLU_ALPHA * g)
        _store_row_tiles(o_ref, _dot(a.astype(jnp.bfloat16), wd_bf[...]) + bd_ref[0])

    @pl.when(i >= nu_ref[0])
    def _():
        o_ref[...] = jnp.zeros_like(o_ref)


def _experts(block_e, n_used, x_sorted, wg, bg, wu, bu, wd, bd):
    d, f = wg.shape[1], wg.shape[2]
    blk_rows = MOE_TILE * ROW_SUB
    rows = lambda i, be, nu: (jnp.minimum(i, nu[0] - 1), 0)
    wsel = lambda i, be, nu: (be[i], 0, 0)
    return pl.pallas_call(
        _expert_kernel,
        grid_spec=pltpu.PrefetchScalarGridSpec(
            num_scalar_prefetch=2,
            grid=(x_sorted.shape[0] // blk_rows,),
            in_specs=[pl.BlockSpec((blk_rows, LANES), rows),
                      pl.BlockSpec((1, d, f), wsel), pl.BlockSpec((1, 1, f), wsel),
                      pl.BlockSpec((1, d, f), wsel), pl.BlockSpec((1, 1, f), wsel),
                      pl.BlockSpec((1, f, d), wsel), pl.BlockSpec((1, 1, d), wsel)],
            out_specs=pl.BlockSpec((blk_rows, LANES), lambda i, be, nu: (i, 0)),
            scratch_shapes=[pltpu.VMEM((d, f), jnp.bfloat16),
                            pltpu.VMEM((d, f), jnp.bfloat16),
                            pltpu.VMEM((f, d), jnp.bfloat16)]),
        out_shape=jax.ShapeDtypeStruct(x_sorted.shape, jnp.float32),
        compiler_params=pltpu.CompilerParams(dimension_semantics=("arbitrary",),
                                             vmem_limit_bytes=VMEM_LIMIT),
        name="experts",
    )(block_e, n_used, x_sorted, wg, bg, wu, bu, wd, bd)


def _moe(groups, counts, wg, bg, wu, bu, wd, bd):
    tm = MOE_TILE
    total = sum(g[4].shape[0] for g in groups)
    cnt = counts.astype(jnp.int32)
    padded = (cnt + tm - 1) // tm * tm
    pad_end = jnp.cumsum(padded)
    pad_start = pad_end - padded
    n_blocks = -(-(total * TOP_K) // tm) + N_EXPERTS
    n_pad = n_blocks * tm
    block_start = jnp.arange(n_blocks, dtype=jnp.int32) * tm
    block_e = jnp.minimum(jnp.sum(block_start[:, None] >= pad_end[None, :], axis=1),
                          N_EXPERTS - 1).astype(jnp.int32)
    n_used = (pad_end[-1:] // tm).astype(jnp.int32)
    experts = jnp.arange(N_EXPERTS, dtype=jnp.int32)

    slots = [rank + jnp.sum(jnp.where(idx[..., None] == experts, pad_start, 0), axis=-1)
             for _, idx, _, rank, _ in groups]
    x_sorted = _dispatch(pad_end, padded, jnp.concatenate(slots, axis=0), [g[0] for g in groups], n_pad)
    out_sorted = _experts(block_e, n_used, x_sorted, wg, bg, wu, bu, wd, bd)
    return [_combine(s, out_sorted, x1, gate) for s, (_, _, gate, _, x1) in zip(slots, groups)]


def kernel(x_prompt, x_sample, cache_k, cache_v, state_pool, attn_norm, w_in, q_norm, k_norm,
           lambda_q1, lambda_k1, lambda_q2, lambda_k2, subln, w_pool, pool_scale, w_out,
           ffn_norm, w_router, b_router, w_gate, b_gate, w_up, b_up, w_down, b_down):
    f32, bf16 = jnp.float32, jnp.bfloat16
    bp, sp, d = x_prompt.shape
    bs, ss, _ = x_sample.shape
    assert d == D_MODEL, "row-tiled expert buffers are laid out for D_MODEL-wide token rows"
    depth = w_in.shape[0]
    past = cache_k.shape[2]
    tp, ts = bp * sp, bs * ss
    xp = x_prompt.reshape(tp, d)
    xs = x_sample.reshape(ts, d)

    blk = _pick_tile(sp, ATT_BLOCK, CHUNK)
    tm_p = _pick_tile(tp, ROW_TILE, blk)
    tm_s = _pick_tile(ts, ROW_TILE)
    group = lax.broadcasted_iota(jnp.int32, (QK_WIDTH, QK_WIDTH), 0) // QK_DIM
    seg = (group == group.T).astype(bf16)
    tri = {tm: (lax.broadcasted_iota(jnp.int32, (tm, tm), 0)
                < lax.broadcasted_iota(jnp.int32, (tm, tm), 1)).astype(bf16) for tm in {tm_p, tm_s}}

    kp_l, vp_l, pp_l, ks_l, vs_l, ps_l = [], [], [], [], [], []
    for l in range(depth):
        lam_init = 0.8 - 0.6 * math.exp(-0.3 * l)
        lam_rows = jnp.stack([lambda_q1[l], lambda_k1[l], lambda_q2[l], lambda_k2[l]]).astype(f32)
        g_attn = attn_norm[l].reshape(1, d)
        w_in_bf = w_in[l].astype(bf16)
        wqkt = w_in_bf[:, :2 * QK_WIDTH].T
        wvu = w_in_bf[:, 2 * QK_WIDTH:]
        qg = jnp.tile(q_norm[l], QK_WIDTH // QK_DIM).astype(f32)
        kg = jnp.tile(k_norm[l], QK_WIDTH // QK_DIM).astype(f32)
        gain_col = jnp.concatenate([qg * (QK_DIM ** -0.5 * LOG2E), kg]).reshape(2 * QK_WIDTH, 1)
        bound = (QK_DIM ** 0.5 * LOG2E * jnp.max(jnp.abs(q_norm[l])) * jnp.max(jnp.abs(k_norm[l]))
                 ).astype(f32).reshape(1, 1)
        sub = subln[l].astype(f32)
        w_pool_bf = w_pool[l].astype(bf16)
        scale = pool_scale[l].reshape(1, POOL_WIDTH)
        wa = w_out[l][:ATTN_WIDTH].astype(bf16)
        wp = w_out[l][ATTN_WIDTH:].astype(bf16)
        g_ffn = ffn_norm[l].reshape(1, d)
        wr_hi, wr_lo = _split_bf16(w_router[l].T.astype(f32))
        br = b_router[l].reshape(N_EXPERTS, 1).astype(f32)
        bg = b_gate[l].reshape(N_EXPERTS, 1, -1)
        bu = b_up[l].reshape(N_EXPERTS, 1, -1)
        bd = b_down[l].reshape(N_EXPERTS, 1, -1)

        qt, kb, vt, k, v, u = _in_proj_prompt(xp, g_attn, wvu, wqkt, gain_col, bp, sp, blk)
        a = _prompt_attention(bound, qt, kb, vt, lam_rows, sub.reshape(V_DIM, 1), bp, sp, blk, lam_init)
        u3 = u.reshape(bp, sp, POOL_WIDTH)
        o_pool = _pool(u3, jnp.zeros((bp, HALO, POOL_WIDTH), f32), w_pool_bf, scale, 0)
        carry0 = jnp.zeros((N_EXPERTS, 128), f32)
        x1p, hp, idx_p, gate_p, rank_p, carry1 = _out_proj(
            a, o_pool.reshape(tp, POOL_WIDTH), xp, wa, wp, g_ffn, wr_hi, wr_lo, br, tri[tm_p], carry0)
        kp_l.append(k.reshape(bp, N_HEADS, 2, QK_DIM, sp).transpose(0, 4, 1, 2, 3))
        vp_l.append(v.reshape(bp, sp, N_HEADS, V_DIM))
        pp_l.append(u3[:, sp - POOL_STATE:])

        q, kb, vb, k, v, u = _in_proj_sample(xs, g_attn, w_in_bf, qg.reshape(1, QK_WIDTH),
                                             kg.reshape(1, QK_WIDTH), seg)
        a = _sample_attention(q, kb, vb, cache_k[l].reshape(bs, past, QK_WIDTH).transpose(0, 2, 1),
                              cache_v[l].reshape(bs, past * N_HEADS, V_DIM), lam_rows, sub.reshape(1, V_DIM),
                              bs, ss, lam_init)
        u3 = u.reshape(bs, ss, POOL_WIDTH)
        hist = jnp.concatenate([jnp.zeros((bs, 1, POOL_WIDTH), f32), state_pool[l]], axis=1)
        o_pool = _pool(u3, hist, w_pool_bf, scale, past)
        x1s, hs, idx_s, gate_s, rank_s, carry2 = _out_proj(
            a, o_pool.reshape(ts, POOL_WIDTH), xs, wa, wp, g_ffn, wr_hi, wr_lo, br, tri[tm_s], carry1)
        ks_l.append(k.reshape(bs, ss, N_HEADS, 2, QK_DIM))
        vs_l.append(v.reshape(bs, ss, N_HEADS, V_DIM))
        ps_l.append(jnp.concatenate([state_pool[l], u3], axis=1)[:, -POOL_STATE:])

        xp, xs = _moe([(hp, idx_p, gate_p, rank_p, x1p), (hs, idx_s, gate_s, rank_s, x1s)],
                      carry2[:, 0], w_gate[l], bg, w_up[l], bu, w_down[l], bd)

    return (xp.reshape(bp, sp, d), xs.reshape(bs, ss, d), jnp.stack(kp_l), jnp.stack(vp_l),
            jnp.stack(pp_l), jnp.stack(ks_l), jnp.stack(vs_l), jnp.stack(ps_l))
```

```python
import functools
import math

import jax
import jax.numpy as jnp
from jax import lax
from jax.experimental import pallas as pl
from jax.experimental.pallas import tpu as pltpu

CHUNK = 64
N_HEADS = 4
QK_DIM = 64
V_DIM = 128
HEAD_W = 2 * QK_DIM
QK_WIDTH = N_HEADS * HEAD_W
ATTN_WIDTH = N_HEADS * V_DIM
POOL_WINDOWS = (2, 4, 8, 16)
POOL_GROUP = 128
POOL_WIDTH = POOL_GROUP * len(POOL_WINDOWS)
POOL_STATE = max(POOL_WINDOWS) - 1
HALO = POOL_STATE + 1
N_EXPERTS = 32
TOP_K = 4
SWIGLU_LIMIT = 7.0
SWIGLU_ALPHA = 1.702
EPS = 1e-6
LOG2E = 1.4426950408889634
D_MODEL = 1024
LANES = 128
ROW_SUB = D_MODEL // LANES

ROW_TILE = 512
ATT_BLOCK = 256
ATT_HEADS = 2
MOE_TILE = 512
DMA_UNROLL = 8
ZERO_ROWS = 64
MAX_FIXED_SHIFT = 60.0
VMEM_LIMIT = 56 * 1024 * 1024


def _pick_tile(n, cap, mult=8):
    t = min(n, cap)
    while n % t or t % mult:
        t -= 1
    return t


def _split_bf16(x):
    hi = x.astype(jnp.bfloat16)
    lo = (x - hi.astype(jnp.float32)).astype(jnp.bfloat16)
    return hi, lo


def _dot_nt(a, b):
    return lax.dot_general(a, b, (((1,), (1,)), ((), ())), preferred_element_type=jnp.float32)


def _dot(a, b):
    return jnp.dot(a, b, preferred_element_type=jnp.float32)


def _store_row_tiles(ref, x):
    m = x.shape[0]
    for c in range(ROW_SUB):
        ref[pl.ds(c, m, stride=ROW_SUB), :] = x[:, c * LANES:(c + 1) * LANES]


def _load_row_tiles(ref, m):
    return jnp.concatenate([ref[pl.ds(c, m, stride=ROW_SUB), :] for c in range(ROW_SUB)], axis=1)


def _normed_bf16(x_ref, g_ref):
    x = x_ref[...]
    ms = jnp.mean(x * x, axis=-1, keepdims=True)
    return (x * lax.rsqrt(ms + EPS) * g_ref[...]).astype(jnp.bfloat16)


def _in_proj_prompt_kernel(x_ref, g_ref, wvu_ref, wqkt_ref, gain_ref,
                           q_ref, kb_ref, vt_ref, k_ref, v_ref, u_ref, *, blk):
    hn = _normed_bf16(x_ref, g_ref)
    tm = hn.shape[0]
    z = _dot(hn, wvu_ref[...])
    v = z[:, :ATTN_WIDTH]
    for c in range(ATTN_WIDTH // LANES):
        v_ref[pl.ds(c, tm, stride=ATTN_WIDTH // LANES), :] = v[:, c * LANES:(c + 1) * LANES]
    u_ref[...] = z[:, ATTN_WIDTH:]

    t = _dot_nt(wqkt_ref[...], hn)
    t3 = t.reshape(2 * QK_WIDTH // QK_DIM, QK_DIM, tm)
    r = lax.rsqrt(jnp.mean(t3 * t3, axis=1, keepdims=True) + EPS)
    tn = (t3 * r).reshape(2 * QK_WIDTH, tm) * gain_ref[...]
    qn, kn = tn[:QK_WIDTH], tn[QK_WIDTH:]
    k_ref[...] = kn
    kb_ref[...] = kn.T.astype(jnp.bfloat16)
    vt = v.T
    for s in range(tm // blk):
        q_ref[s] = qn[:, s * blk:(s + 1) * blk].astype(jnp.bfloat16)
        vt_ref[s] = vt[:, s * blk:(s + 1) * blk].astype(jnp.bfloat16)


def _in_proj_prompt(x, g, wvu, wqkt, gain, batch, seq, blk):
    t, d = x.shape
    tm = _pick_tile(seq, ROW_TILE, blk)
    per_seq = seq // tm
    row = lambda w: pl.BlockSpec((tm, w), lambda i: (i, 0))
    full = lambda a: pl.BlockSpec(a.shape, lambda i: (0, 0))
    slab = pl.BlockSpec((tm // blk, QK_WIDTH, blk), lambda i: (i, 0, 0))
    f32, bf16 = jnp.float32, jnp.bfloat16
    kern = functools.partial(_in_proj_prompt_kernel, blk=blk)
    return pl.pallas_call(
        kern,
        grid=(t // tm,),
        in_specs=[row(d), full(g), full(wvu), full(wqkt), full(gain)],
        out_specs=[slab, row(QK_WIDTH), slab,
                   pl.BlockSpec((None, QK_WIDTH, tm), lambda i: (i // per_seq, 0, i % per_seq)),
                   pl.BlockSpec((tm * N_HEADS, V_DIM), lambda i: (i, 0)),
                   row(POOL_WIDTH)],
        out_shape=[jax.ShapeDtypeStruct((t // blk, QK_WIDTH, blk), bf16),
                   jax.ShapeDtypeStruct((t, QK_WIDTH), bf16),
                   jax.ShapeDtypeStruct((t // blk, ATTN_WIDTH, blk), bf16),
                   jax.ShapeDtypeStruct((batch, QK_WIDTH, seq), f32),
                   jax.ShapeDtypeStruct((t * N_HEADS, V_DIM), f32),
                   jax.ShapeDtypeStruct((t, POOL_WIDTH), f32)],
        compiler_params=pltpu.CompilerParams(dimension_semantics=("arbitrary",),
                                             vmem_limit_bytes=VMEM_LIMIT),
        name="in_proj_prompt",
    )(x, g, wvu, wqkt, gain)


def _in_proj_sample_kernel(x_ref, g_ref, w_ref, qg_ref, kg_ref, seg_ref,
                           q_ref, kb_ref, vb_ref, k_ref, v_ref, u_ref):
    hn = _normed_bf16(x_ref, g_ref)
    z = _dot(hn, w_ref[...])
    seg = seg_ref[...]

    def qk_norm(t, gain):
        hi, lo = _split_bf16(t * t)
        ss = _dot(hi, seg) + _dot(lo, seg)
        return t * lax.rsqrt(ss * (1.0 / QK_DIM) + EPS) * gain

    qn = qk_norm(z[:, :QK_WIDTH], qg_ref[...])
    kn = qk_norm(z[:, QK_WIDTH:2 * QK_WIDTH], kg_ref[...])
    v = z[:, 2 * QK_WIDTH:2 * QK_WIDTH + ATTN_WIDTH]
    q_ref[...] = (qn * (QK_DIM ** -0.5)).astype(jnp.bfloat16)
    k_ref[...] = kn
    kb_ref[...] = kn.astype(jnp.bfloat16)
    v_ref[...] = v
    vb_ref[...] = v.astype(jnp.bfloat16)
    u_ref[...] = z[:, 2 * QK_WIDTH + ATTN_WIDTH:]


def _in_proj_sample(x, g, w_bf, qg, kg, seg):
    t, d = x.shape
    tm = _pick_tile(t, ROW_TILE)
    row = lambda w: pl.BlockSpec((tm, w), lambda i: (i, 0))
    full = lambda a: pl.BlockSpec(a.shape, lambda i: (0, 0))
    f32, bf16 = jnp.float32, jnp.bfloat16
    return pl.pallas_call(
        _in_proj_sample_kernel,
        grid=(t // tm,),
        in_specs=[row(d), full(g), full(w_bf), full(qg), full(kg), full(seg)],
        out_specs=[row(QK_WIDTH)] * 2 + [row(ATTN_WIDTH), row(QK_WIDTH), row(ATTN_WIDTH), row(POOL_WIDTH)],
        out_shape=[jax.ShapeDtypeStruct((t, QK_WIDTH), bf16),
                   jax.ShapeDtypeStruct((t, QK_WIDTH), bf16),
                   jax.ShapeDtypeStruct((t, ATTN_WIDTH), bf16),
                   jax.ShapeDtypeStruct((t, QK_WIDTH), f32),
                   jax.ShapeDtypeStruct((t, ATTN_WIDTH), f32),
                   jax.ShapeDtypeStruct((t, POOL_WIDTH), f32)],
        compiler_params=pltpu.CompilerParams(dimension_semantics=("arbitrary",),
                                             vmem_limit_bytes=VMEM_LIMIT),
        name="in_proj_sample",
    )(x, g, w_bf, qg, kg, seg)


def _lambda_scalar(lam_ref, lam_init):
    l = lam_ref[...]
    a = jnp.sum(l[0:1] * l[1:2], axis=-1, keepdims=True)
    b = jnp.sum(l[2:3] * l[3:4], axis=-1, keepdims=True)
    return jnp.exp(a) - jnp.exp(b) + lam_init


def _split_maps(q, axis):
    pos = lax.broadcasted_iota(jnp.int32, q.shape, axis)
    zero = jnp.zeros_like(q)
    return jnp.where(pos < QK_DIM, q, zero), jnp.where(pos >= QK_DIM, q, zero)


def _prompt_attn_kernel(bound_ref, lam_ref, subln_ref, *refs, lam_init, blk, heads):
    q_refs, k_refs, v_refs = refs[:heads], refs[heads:2 * heads], refs[2 * heads:3 * heads]
    o_ref, s_even, s_odd, p_even, p_odd, acc_ref = refs[3 * heads:]
    i = pl.program_id(2)
    bound = bound_ref[0, 0]
    lam = _lambda_scalar(lam_ref, lam_init)
    n_chain = 2 * heads
    qs = []
    for h in range(heads):
        qs.extend(_split_maps(q_refs[h][...], 0))

    def scores(j, s_buf):
        rows = pl.ds(pl.multiple_of(j * blk, blk), blk)
        for h in range(heads):
            kb = k_refs[h][rows, :]
            for c in range(2):
                s_buf[2 * h + c] = _dot(kb, qs[2 * h + c])

    def stage(carry, s_cur, p_prev, v_prev, p_cur, masked, nxt, fixed):
        if nxt is not None:
            scores(*nxt)
        out = []
        for n in range(n_chain):
            h = n // 2
            m, l = carry[n]
            pv = _dot(v_refs[h][v_prev], p_prev[n])
            s = s_cur[n]
            if masked:
                kc = lax.broadcasted_iota(jnp.int32, s.shape, 0) // CHUNK
                qc = lax.broadcasted_iota(jnp.int32, s.shape, 1) // CHUNK
                s = jnp.where(kc <= qc, s, -jnp.inf)
            if fixed:
                p = jnp.exp2(s - bound)
                acc_ref[n] = acc_ref[n] + pv
                out.append((m, l + jnp.sum(p, axis=0, keepdims=True)))
            else:
                m_new = jnp.maximum(m, jnp.max(s, axis=0, keepdims=True))
                alpha = jnp.exp2(m - m_new)
                p = jnp.exp2(s - m_new)
                acc_ref[n] = (acc_ref[n] + pv) * alpha
                out.append((m_new, alpha * l + jnp.sum(p, axis=0, keepdims=True)))
            p_cur[n] = p.astype(jnp.bfloat16)
        return tuple(out)

    def finish(carry, p_last):
        for h in range(heads):
            vb = v_refs[h][i]
            o = [(acc_ref[2 * h + c] + _dot(vb, p_last[2 * h + c])) / carry[2 * h + c][1] for c in range(2)]
            o = o[0] - lam * o[1]
            ms = jnp.mean(o * o, axis=0, keepdims=True)
            o = o * lax.rsqrt(ms + EPS) * (subln_ref[...] * (1.0 - lam_init))
            o_ref[:, h * V_DIM:(h + 1) * V_DIM] = o.T.astype(o_ref.dtype)

    def run(fixed):
        p_odd[...] = jnp.zeros_like(p_odd)
        acc_ref[...] = jnp.zeros_like(acc_ref)
        scores(0, s_even)
        neg = jnp.full((1, blk), -jnp.inf, jnp.float32)
        zl = jnp.zeros((1, blk), jnp.float32)

        def pair(jj, carry):
            j = 2 * jj
            carry = stage(carry, s_even, p_odd, jnp.maximum(j - 1, 0), p_even, False, (j + 1, s_odd), fixed)
            return stage(carry, s_odd, p_even, j, p_odd, False, (j + 2, s_even), fixed)

        carry = lax.fori_loop(0, i // 2, pair, ((neg, zl),) * n_chain)

        @pl.when(i % 2 == 0)
        def _():
            finish(stage(carry, s_even, p_odd, jnp.maximum(i - 1, 0), p_even, True, None, fixed), p_even)

        @pl.when(i % 2 == 1)
        def _():
            c = stage(carry, s_even, p_odd, jnp.maximum(i - 2, 0), p_even, False, (i, s_odd), fixed)
            finish(stage(c, s_odd, p_even, i - 1, p_odd, True, None, fixed), p_odd)

    small = bound <= MAX_FIXED_SHIFT
    pl.when(small)(functools.partial(run, True))
    pl.when(jnp.logical_not(small))(functools.partial(run, False))


def _prompt_attention(bound, qt, k, vt, lam_rows, subln_col, batch, seq, blk, lam_init):
    nq = seq // blk
    t = batch * seq
    heads = ATT_HEADS
    kern = functools.partial(_prompt_attn_kernel, lam_init=lam_init, blk=blk, heads=heads)
    s_buf = pltpu.VMEM((2 * heads, blk, blk), jnp.float32)
    p_buf = pltpu.VMEM((2 * heads, blk, blk), jnp.bfloat16)
    return pl.pallas_call(
        kern,
        grid=(batch, N_HEADS // heads, nq),
        in_specs=[pl.BlockSpec(memory_space=pltpu.SMEM),
                  pl.BlockSpec(lam_rows.shape, lambda b, h, i: (0, 0)),
                  pl.BlockSpec(subln_col.shape, lambda b, h, i: (0, 0))]
        + [pl.BlockSpec((None, HEAD_W, blk), lambda b, h, i, e=e: (b * nq + i, h * heads + e, 0)) for e in range(heads)]
        + [pl.BlockSpec((seq, HEAD_W), lambda b, h, i, e=e: (b, h * heads + e)) for e in range(heads)]
        + [pl.BlockSpec((nq, V_DIM, blk), lambda b, h, i, e=e: (b, h * heads + e, 0)) for e in range(heads)],
        out_specs=pl.BlockSpec((blk, heads * V_DIM), lambda b, h, i: (b * nq + i, h)),
        out_shape=jax.ShapeDtypeStruct((t, ATTN_WIDTH), jnp.bfloat16),
        scratch_shapes=[s_buf, s_buf, p_buf, p_buf, pltpu.VMEM((2 * heads, V_DIM, blk), jnp.float32)],
        compiler_params=pltpu.CompilerParams(dimension_semantics=("arbitrary",) * 3,
                                             vmem_limit_bytes=VMEM_LIMIT),
        name="prompt_attention",
    )(bound, lam_rows, subln_col, *([qt] * heads + [k] * heads + [vt] * heads))


def _sample_attn_kernel(lam_ref, subln_ref, q_ref, kn_ref, vn_ref, ck_ref, cv_ref, o_ref, *, lam_init, past):
    lam = _lambda_scalar(lam_ref, lam_init)
    q = q_ref[...]
    nq = q.shape[0]
    qpos = past + lax.broadcasted_iota(jnp.int32, (nq, nq), 0)
    kpos = past + lax.broadcasted_iota(jnp.int32, (nq, nq), 1)
    vis_new = (kpos // CHUNK) <= (qpos // CHUNK)
    for h in range(N_HEADS):
        qk = slice(h * HEAD_W, (h + 1) * HEAD_W)
        vv = slice(h * V_DIM, (h + 1) * V_DIM)
        q1, q2 = _split_maps(q[:, qk], 1)
        kc = ck_ref[0, qk, :].astype(jnp.bfloat16)
        vc = cv_ref[0, pl.ds(h, past, stride=N_HEADS), :].astype(jnp.bfloat16)
        kn = kn_ref[:, qk]
        vn = vn_ref[:, vv]

        def one_map(qm):
            sc = _dot(qm, kc)
            sn = jnp.where(vis_new, _dot_nt(qm, kn), -jnp.inf)
            m = jnp.maximum(jnp.max(sc, axis=-1, keepdims=True), jnp.max(sn, axis=-1, keepdims=True))
            pc = jnp.exp(sc - m)
            pn = jnp.exp(sn - m)
            l = jnp.sum(pc, axis=-1, keepdims=True) + jnp.sum(pn, axis=-1, keepdims=True)
            return (_dot(pc.astype(jnp.bfloat16), vc) + _dot(pn.astype(jnp.bfloat16), vn)) / l

        o = one_map(q1) - lam * one_map(q2)
        ms = jnp.mean(o * o, axis=-1, keepdims=True)
        o_ref[:, vv] = (o * lax.rsqrt(ms + EPS) * subln_ref[...] * (1.0 - lam_init)).astype(o_ref.dtype)


def _sample_attention(q, k_new, v_new, cache_k, cache_v, lam_rows, subln, batch, nq, lam_init):
    past = cache_k.shape[2]
    kern = functools.partial(_sample_attn_kernel, lam_init=lam_init, past=past)
    row = lambda w: pl.BlockSpec((nq, w), lambda b: (b, 0))
    return pl.pallas_call(
        kern,
        grid=(batch,),
        in_specs=[pl.BlockSpec(lam_rows.shape, lambda b: (0, 0)),
                  pl.BlockSpec(subln.shape, lambda b: (0, 0)),
                  row(QK_WIDTH), row(QK_WIDTH), row(ATTN_WIDTH),
                  pl.BlockSpec((1, QK_WIDTH, past), lambda b: (b, 0, 0)),
                  pl.BlockSpec((1, past * N_HEADS, V_DIM), lambda b: (b, 0, 0))],
        out_specs=row(ATTN_WIDTH),
        out_shape=jax.ShapeDtypeStruct((batch * nq, ATTN_WIDTH), jnp.bfloat16),
        compiler_params=pltpu.CompilerParams(dimension_semantics=("arbitrary",),
                                             vmem_limit_bytes=VMEM_LIMIT),
        name="sample_attention",
    )(lam_rows, subln, q, k_new, v_new, cache_k, cache_v)


def _pool_kernel(u_ref, prev_ref, hist_ref, w_ref, scale_ref, o_ref, ext_ref, *, pos0, tq):
    j = pl.program_id(1)
    ext_ref[0:HALO, :] = jnp.where(j == 0, hist_ref[0], prev_ref[0])
    ext_ref[HALO:, :] = u_ref[0]
    pos = pos0 + j * tq + lax.broadcasted_iota(jnp.int32, (tq, 1), 0)
    for g, w in enumerate(POOL_WINDOWS):
        cols = slice(g * POOL_GROUP, (g + 1) * POOL_GROUP)
        acc = ext_ref[HALO:, cols]
        for back in range(1, w):
            acc = acc + ext_ref[HALO - back:HALO - back + tq, cols]
        count = jnp.minimum(pos + 1, w).astype(jnp.float32)
        d = acc / count - ext_ref[HALO:, cols]
        y = _dot(d.astype(jnp.bfloat16), w_ref[g])
        o_ref[0, :, cols] = (y * scale_ref[:, cols]).astype(o_ref.dtype)


def _pool(u, hist, w_pool_bf, scale, pos0):
    b, q, c = u.shape
    tq = _pick_tile(q, ROW_TILE, HALO)
    per = tq // HALO
    kern = functools.partial(_pool_kernel, pos0=pos0, tq=tq)
    return pl.pallas_call(
        kern,
        grid=(b, q // tq),
        in_specs=[pl.BlockSpec((1, tq, c), lambda i, j: (i, j, 0)),
                  pl.BlockSpec((1, HALO, c), lambda i, j: (i, jnp.maximum(j * per - 1, 0), 0)),
                  pl.BlockSpec((1, HALO, c), lambda i, j: (i, 0, 0)),
                  pl.BlockSpec(w_pool_bf.shape, lambda i, j: (0, 0, 0)),
                  pl.BlockSpec(scale.shape, lambda i, j: (0, 0))],
        out_specs=pl.BlockSpec((1, tq, c), lambda i, j: (i, j, 0)),
        out_shape=jax.ShapeDtypeStruct((b, q, c), jnp.bfloat16),
        scratch_shapes=[pltpu.VMEM((HALO + tq, c), jnp.float32)],
        compiler_params=pltpu.CompilerParams(dimension_semantics=("arbitrary",) * 2,
                                             vmem_limit_bytes=VMEM_LIMIT),
        name="pool",
    )(u, u, hist, w_pool_bf, scale)


def _out_proj_kernel(a_ref, p_ref, x_ref, wa_ref, wp_ref, g_ref, wr_hi_ref, wr_lo_ref, br_ref,
                     tri_ref, cin_ref,
                     x1_ref, h_ref, idx_ref, gate_ref, rank_ref, cout_ref, carry_ref):
    i = pl.program_id(0)

    @pl.when(i == 0)
    def _():
        carry_ref[...] = cin_ref[...]

    x1 = x_ref[...] + _dot(a_ref[...], wa_ref[...]) + _dot(p_ref[...], wp_ref[...])
    x1_ref[...] = x1
    ms = jnp.mean(x1 * x1, axis=-1, keepdims=True)
    h = x1 * lax.rsqrt(ms + EPS) * g_ref[...]
    _store_row_tiles(h_ref, h)

    h_hi, h_lo = _split_bf16(h)
    wr_hi = wr_hi_ref[...]
    logits = _dot_nt(wr_hi, h_hi) + _dot_nt(wr_hi, h_lo) + _dot_nt(wr_lo_ref[...], h_hi) + br_ref[...]
    expert = lax.broadcasted_iota(jnp.int32, logits.shape, 0)
    work = logits
    vals, idxs, hots = [], [], []
    for _ in range(TOP_K):
        m = jnp.max(work, axis=0, keepdims=True)
        sel = jnp.min(jnp.where(work == m, expert, N_EXPERTS), axis=0, keepdims=True)
        hot = expert == sel
        work = jnp.where(hot, -jnp.inf, work)
        vals.append(m)
        idxs.append(sel)
        hots.append(hot)
    ex = [jnp.exp(v - vals[0]) for v in vals]
    den = ex[0] + ex[1] + ex[2] + ex[3]
    gate_ref[...] = jnp.concatenate([e / den for e in ex] + [jnp.zeros_like(den)] * (8 - TOP_K), axis=0)
    idx_ref[...] = jnp.concatenate(idxs, axis=1)

    hot_all = hots[0] | hots[1] | hots[2] | hots[3]
    hot_f = jnp.where(hot_all, 1.0, 0.0)
    prefix = _dot(hot_f.astype(jnp.bfloat16), tri_ref[...]) + carry_ref[:, 0:1]
    ranks = [jnp.sum(jnp.where(hot, prefix, 0.0), axis=0, keepdims=True) for hot in hots]
    rank_ref[...] = jnp.concatenate(ranks, axis=1).astype(jnp.int32)
    carry_ref[...] = carry_ref[...] + jnp.sum(hot_f, axis=1, keepdims=True)
    cout_ref[...] = carry_ref[...]


def _out_proj(a, p, x, wa, wp, g, wr_hi, wr_lo, br, tri, carry_in):
    t, d = x.shape
    tm = tri.shape[0]
    row = lambda w: pl.BlockSpec((tm, w), lambda i: (i, 0))
    lst = pl.BlockSpec((None, 1, TOP_K * tm), lambda i: (i, 0, 0))
    full = lambda arr: pl.BlockSpec(arr.shape, lambda i: (0, 0))
    f32, i32 = jnp.float32, jnp.int32
    n_tiles = t // tm
    return pl.pallas_call(
        _out_proj_kernel,
        grid=(n_tiles,),
        in_specs=[row(ATTN_WIDTH), row(POOL_WIDTH), row(d), full(wa), full(wp), full(g),
                  full(wr_hi), full(wr_lo), full(br), full(tri), full(carry_in)],
        out_specs=[row(d), pl.BlockSpec((tm * ROW_SUB, LANES), lambda i: (i, 0)), lst,
                   pl.BlockSpec((8, tm), lambda i: (0, i)), lst, full(carry_in)],
        out_shape=[jax.ShapeDtypeStruct((t, d), f32),
                   jax.ShapeDtypeStruct((t * ROW_SUB, LANES), f32),
                   jax.ShapeDtypeStruct((n_tiles, 1, TOP_K * tm), i32),
                   jax.ShapeDtypeStruct((8, t), f32),
                   jax.ShapeDtypeStruct((n_tiles, 1, TOP_K * tm), i32),
                   jax.ShapeDtypeStruct(carry_in.shape, f32)],
        scratch_shapes=[pltpu.VMEM(carry_in.shape, f32)],
        compiler_params=pltpu.CompilerParams(dimension_semantics=("arbitrary",),
                                             vmem_limit_bytes=VMEM_LIMIT),
        name="out_proj_router",
    )(a, p, x, wa, wp, g, wr_hi, wr_lo, br, tri, carry_in)


def _fetch_slots(slots_hbm, slots_smem, sem, tile):
    return pltpu.make_async_copy(slots_hbm.at[tile, 0], slots_smem, sem)


def _row_tile(ref, row):
    return ref.at[pl.ds(pl.multiple_of(row * ROW_SUB, ROW_SUB), ROW_SUB)]


def _row_tile_span(ref, row, rows):
    return ref.at[pl.ds(pl.multiple_of(row * ROW_SUB, ROW_SUB), rows * ROW_SUB)]


def _dispatch_kernel(pend_ref, padded_ref, slots_hbm, *rest, tm, group_tiles, n_blocks):
    n_groups = len(group_tiles)
    h_refs = rest[:n_groups]
    xs_out, slots_smem, zeros, slot_sem, row_sem = rest[n_groups:]
    i = pl.program_id(0)
    n = pl.num_programs(0)
    zr = zeros.shape[0] // ROW_SUB
    per_block = MOE_TILE // zr

    def zero_block(first_row):
        for c in range(per_block):
            pltpu.make_async_copy(zeros, _row_tile_span(xs_out, first_row + c * zr, zr), row_sem).start()

    def wait_zero_blocks(count):
        def body(_, c):
            pltpu.make_async_copy(zeros, _row_tile_span(xs_out, 0, zr), row_sem).wait()
            return c
        lax.fori_loop(0, count * per_block, body, 0)

    @pl.when(i == 0)
    def _():
        _fetch_slots(slots_hbm, slots_smem, slot_sem, 0).start()
        zeros[...] = jnp.zeros_like(zeros)
        for e in range(N_EXPERTS):
            @pl.when(padded_ref[e] > 0)
            def _():
                zero_block(pend_ref[e] - MOE_TILE)
                wait_zero_blocks(1)
        used = pend_ref[N_EXPERTS - 1] // MOE_TILE

        def spare(b, c):
            zero_block(b * MOE_TILE)
            return c
        lax.fori_loop(used, n_blocks, spare, 0)
        wait_zero_blocks(n_blocks - used)

    _fetch_slots(slots_hbm, slots_smem, slot_sem, i).wait()

    def scatter(h_ref):
        def issue(g, c):
            for rr in range(DMA_UNROLL):
                r = g * DMA_UNROLL + rr
                for k in range(TOP_K):
                    dst = slots_smem[k * tm + r]
                    pltpu.make_async_copy(_row_tile(h_ref, r), _row_tile(xs_out, dst), row_sem).start(priority=k % 2)
            return c

        lax.fori_loop(0, tm // DMA_UNROLL, issue, 0)

        @pl.when(i + 1 < n)
        def _():
            _fetch_slots(slots_hbm, slots_smem, slot_sem, i + 1).start()

        for _ in range(TOP_K):
            pltpu.make_async_copy(h_ref, _row_tile_span(xs_out, 0, tm), row_sem).wait()

    first = 0
    for h_ref, tiles in zip(h_refs, group_tiles):
        pl.when((i >= first) & (i < first + tiles))(functools.partial(scatter, h_ref))
        first += tiles


def _dispatch(pad_end, padded, slots, h_groups, n_pad):
    n_tiles, _, per = slots.shape
    tm = per // TOP_K
    group_tiles = tuple(h.shape[0] // (tm * ROW_SUB) for h in h_groups)
    assert sum(group_tiles) == n_tiles
    kern = functools.partial(_dispatch_kernel, tm=tm, group_tiles=group_tiles, n_blocks=n_pad // MOE_TILE)
    any_spec = pl.BlockSpec(memory_space=pl.ANY)
    in_specs, first = [any_spec], 0
    for tiles in group_tiles:
        in_specs.append(pl.BlockSpec(
            (tm * ROW_SUB, LANES),
            lambda i, pe, pd, first=first, tiles=tiles: (jnp.clip(i - first, 0, tiles - 1), 0)))
        first += tiles
    return pl.pallas_call(
        kern,
        grid_spec=pltpu.PrefetchScalarGridSpec(
            num_scalar_prefetch=2,
            grid=(n_tiles,),
            in_specs=in_specs,
            out_specs=any_spec,
            scratch_shapes=[pltpu.SMEM((per,), jnp.int32),
                            pltpu.VMEM((ZERO_ROWS * ROW_SUB, LANES), jnp.float32),
                            pltpu.SemaphoreType.DMA(()), pltpu.SemaphoreType.DMA(())]),
        out_shape=jax.ShapeDtypeStruct((n_pad * ROW_SUB, LANES), jnp.float32),
        compiler_params=pltpu.CompilerParams(dimension_semantics=("arbitrary",),
                                             vmem_limit_bytes=VMEM_LIMIT),
        name="dispatch",
    )(pad_end, padded, slots, *h_groups)


def _combine_kernel(slots_hbm, os_hbm, x1_ref, gate_ref, y_ref, slots_smem, buf_a, buf_b,
                    slot_sem, sem_a, sem_b, *, tm):
    i = pl.program_id(0)
    n = pl.num_programs(0)

    def gather(buf, sem):
        _fetch_slots(slots_hbm, slots_smem, slot_sem, 0).wait()

        def issue(g, c):
            for rr in range(DMA_UNROLL):
                r = g * DMA_UNROLL + rr
                for k in range(TOP_K):
                    src = slots_smem[k * tm + r]
                    pltpu.make_async_copy(_row_tile(os_hbm, src), _row_tile(buf.at[k], r), sem).start(priority=k % 2)
            return c

        lax.fori_loop(0, tm // DMA_UNROLL, issue, 0)

    def gather_next(buf, sem):
        @pl.when(i + 1 < n)
        def _():
            gather(buf, sem)

            @pl.when(i + 2 < n)
            def _():
                _fetch_slots(slots_hbm, slots_smem, slot_sem, i + 2).start()

    def reduce(buf, sem):
        for k in range(TOP_K):
            pltpu.make_async_copy(_row_tile_span(os_hbm, 0, tm), buf.at[k], sem).wait()
        gate = gate_ref[...].T
        for c in range(ROW_SUB):
            cols = slice(c * LANES, (c + 1) * LANES)
            y = x1_ref[:, cols]
            for k in range(TOP_K):
                y = y + gate[:, k:k + 1] * buf[k, pl.ds(c, tm, stride=ROW_SUB), :]
            y_ref[:, cols] = y

    @pl.when(i == 0)
    def _():
        _fetch_slots(slots_hbm, slots_smem, slot_sem, 0).start()
        gather(buf_a, sem_a)

        @pl.when(n > 1)
        def _():
            _fetch_slots(slots_hbm, slots_smem, slot_sem, 1).start()

    @pl.when(i % 2 == 0)
    def _():
        gather_next(buf_b, sem_b)
        reduce(buf_a, sem_a)

    @pl.when(i % 2 == 1)
    def _():
        gather_next(buf_a, sem_a)
        reduce(buf_b, sem_b)


def _combine(slots, out_sorted, x1, gate_rows):
    n_tiles, _, per = slots.shape
    tm = per // TOP_K
    t, d = x1.shape
    kern = functools.partial(_combine_kernel, tm=tm)
    buf = pltpu.VMEM((TOP_K, tm * ROW_SUB, LANES), jnp.float32)
    dma = pltpu.SemaphoreType.DMA(())
    return pl.pallas_call(
        kern,
        grid=(n_tiles,),
        in_specs=[pl.BlockSpec(memory_space=pl.ANY),
                  pl.BlockSpec(memory_space=pl.ANY),
                  pl.BlockSpec((tm, d), lambda i: (i, 0)),
                  pl.BlockSpec((8, tm), lambda i: (0, i))],
        out_specs=pl.BlockSpec((tm, d), lambda i: (i, 0)),
        out_shape=jax.ShapeDtypeStruct((t, d), jnp.float32),
        scratch_shapes=[pltpu.SMEM((per,), jnp.int32), buf, buf, dma, dma, dma],
        compiler_params=pltpu.CompilerParams(dimension_semantics=("arbitrary",),
                                             vmem_limit_bytes=VMEM_LIMIT),
        name="combine",
    )(slots, out_sorted, x1, gate_rows)


def _expert_kernel(be_ref, nu_ref, x_ref, wg_ref, bg_ref, wu_ref, bu_ref, wd_ref, bd_ref, o_ref,
                   wg_bf, wu_bf, wd_bf):
    i = pl.program_id(0)
    tm = x_ref.shape[0] // ROW_SUB

    @pl.when(i < nu_ref[0])
    def _():
        @pl.when((i == 0) | (be_ref[i] != be_ref[jnp.maximum(i - 1, 0)]))
        def _():
            wg_bf[...] = wg_ref[0].astype(jnp.bfloat16)
            wu_bf[...] = wu_ref[0].astype(jnp.bfloat16)
            wd_bf[...] = wd_ref[0].astype(jnp.bfloat16)

        x = _load_row_tiles(x_ref, tm).astype(jnp.bfloat16)
        g = jnp.minimum(_dot(x, wg_bf[...]) + bg_ref[0], SWIGLU_LIMIT)
        u = jnp.clip(_dot(x, wu_bf[...]) + bu_ref[0], -SWIGLU_LIMIT, SWIGLU_LIMIT)
        a = (u + 1.0) * g * jax.nn.sigmoid(SWIGLU_ALPHA * g)
        _store_row_tiles(o_ref, _dot(a.astype(jnp.bfloat16), wd_bf[...]) + bd_ref[0])

    @pl.when(i >= nu_ref[0])
    def _():
        o_ref[...] = jnp.zeros_like(o_ref)


def _experts(block_e, n_used, x_sorted, wg, bg, wu, bu, wd, bd):
    d, f = wg.shape[1], wg.shape[2]
    blk_rows = MOE_TILE * ROW_SUB
    rows = lambda i, be, nu: (jnp.minimum(i, nu[0] - 1), 0)
    wsel = lambda i, be, nu: (be[i], 0, 0)
    return pl.pallas_call(
        _expert_kernel,
        grid_spec=pltpu.PrefetchScalarGridSpec(
            num_scalar_prefetch=2,
            grid=(x_sorted.shape[0] // blk_rows,),
            in_specs=[pl.BlockSpec((blk_rows, LANES), rows),
                      pl.BlockSpec((1, d, f), wsel), pl.BlockSpec((1, 1, f), wsel),
                      pl.BlockSpec((1, d, f), wsel), pl.BlockSpec((1, 1, f), wsel),
                      pl.BlockSpec((1, f, d), wsel), pl.BlockSpec((1, 1, d), wsel)],
            out_specs=pl.BlockSpec((blk_rows, LANES), lambda i, be, nu: (i, 0)),
            scratch_shapes=[pltpu.VMEM((d, f), jnp.bfloat16),
                            pltpu.VMEM((d, f), jnp.bfloat16),
                            pltpu.VMEM((f, d), jnp.bfloat16)]),
        out_shape=jax.ShapeDtypeStruct(x_sorted.shape, jnp.float32),
        compiler_params=pltpu.CompilerParams(dimension_semantics=("arbitrary",),
                                             vmem_limit_bytes=VMEM_LIMIT),
        name="experts",
    )(block_e, n_used, x_sorted, wg, bg, wu, bu, wd, bd)


def _moe(groups, counts, wg, bg, wu, bu, wd, bd):
    tm = MOE_TILE
    total = sum(g[4].shape[0] for g in groups)
    cnt = counts.astype(jnp.int32)
    padded = (cnt + tm - 1) // tm * tm
    pad_end = jnp.cumsum(padded)
    pad_start = pad_end - padded
    n_blocks = -(-(total * TOP_K) // tm) + N_EXPERTS
    n_pad = n_blocks * tm
    block_start = jnp.arange(n_blocks, dtype=jnp.int32) * tm
    block_e = jnp.minimum(jnp.sum(block_start[:, None] >= pad_end[None, :], axis=1),
                          N_EXPERTS - 1).astype(jnp.int32)
    n_used = (pad_end[-1:] // tm).astype(jnp.int32)
    experts = jnp.arange(N_EXPERTS, dtype=jnp.int32)

    slots = [rank + jnp.sum(jnp.where(idx[..., None] == experts, pad_start, 0), axis=-1)
             for _, idx, _, rank, _ in groups]
    x_sorted = _dispatch(pad_end, padded, jnp.concatenate(slots, axis=0), [g[0] for g in groups], n_pad)
    out_sorted = _experts(block_e, n_used, x_sorted, wg, bg, wu, bu, wd, bd)
    return [_combine(s, out_sorted, x1, gate) for s, (_, _, gate, _, x1) in zip(slots, groups)]


def kernel(x_prompt, x_sample, cache_k, cache_v, state_pool, attn_norm, w_in, q_norm, k_norm,
           lambda_q1, lambda_k1, lambda_q2, lambda_k2, subln, w_pool, pool_scale, w_out,
           ffn_norm, w_router, b_router, w_gate, b_gate, w_up, b_up, w_down, b_down):
    f32, bf16 = jnp.float32, jnp.bfloat16
    bp, sp, d = x_prompt.shape
    bs, ss, _ = x_sample.shape
    assert d == D_MODEL, "row-tiled expert buffers are laid out for D_MODEL-wide token rows"
    depth = w_in.shape[0]
    past = cache_k.shape[2]
    tp, ts = bp * sp, bs * ss
    xp = x_prompt.reshape(tp, d)
    xs = x_sample.reshape(ts, d)

    blk = _pick_tile(sp, ATT_BLOCK, CHUNK)
    tm_p = _pick_tile(tp, ROW_TILE, blk)
    tm_s = _pick_tile(ts, ROW_TILE)
    group = lax.broadcasted_iota(jnp.int32, (QK_WIDTH, QK_WIDTH), 0) // QK_DIM
    seg = (group == group.T).astype(bf16)
    tri = {tm: (lax.broadcasted_iota(jnp.int32, (tm, tm), 0)
                < lax.broadcasted_iota(jnp.int32, (tm, tm), 1)).astype(bf16) for tm in {tm_p, tm_s}}

    kp_l, vp_l, pp_l, ks_l, vs_l, ps_l = [], [], [], [], [], []
    for l in range(depth):
        lam_init = 0.8 - 0.6 * math.exp(-0.3 * l)
        lam_rows = jnp.stack([lambda_q1[l], lambda_k1[l], lambda_q2[l], lambda_k2[l]]).astype(f32)
        g_attn = attn_norm[l].reshape(1, d)
        w_in_bf = w_in[l].astype(bf16)
        wqkt = w_in_bf[:, :2 * QK_WIDTH].T
        wvu = w_in_bf[:, 2 * QK_WIDTH:]
        qg = jnp.tile(q_norm[l], QK_WIDTH // QK_DIM).astype(f32)
        kg = jnp.tile(k_norm[l], QK_WIDTH // QK_DIM).astype(f32)
        gain_col = jnp.concatenate([qg * (QK_DIM ** -0.5 * LOG2E), kg]).reshape(2 * QK_WIDTH, 1)
        bound = (QK_DIM ** 0.5 * LOG2E * jnp.max(jnp.abs(q_norm[l])) * jnp.max(jnp.abs(k_norm[l]))
                 ).astype(f32).reshape(1, 1)
        sub = subln[l].astype(f32)
        w_pool_bf = w_pool[l].astype(bf16)
        scale = pool_scale[l].reshape(1, POOL_WIDTH)
        wa = w_out[l][:ATTN_WIDTH].astype(bf16)
        wp = w_out[l][ATTN_WIDTH:].astype(bf16)
        g_ffn = ffn_norm[l].reshape(1, d)
        wr_hi, wr_lo = _split_bf16(w_router[l].T.astype(f32))
        br = b_router[l].reshape(N_EXPERTS, 1).astype(f32)
        bg = b_gate[l].reshape(N_EXPERTS, 1, -1)
        bu = b_up[l].reshape(N_EXPERTS, 1, -1)
        bd = b_down[l].reshape(N_EXPERTS, 1, -1)

        qt, kb, vt, k, v, u = _in_proj_prompt(xp, g_attn, wvu, wqkt, gain_col, bp, sp, blk)
        a = _prompt_attention(bound, qt, kb, vt, lam_rows, sub.reshape(V_DIM, 1), bp, sp, blk, lam_init)
        u3 = u.reshape(bp, sp, POOL_WIDTH)
        o_pool = _pool(u3, jnp.zeros((bp, HALO, POOL_WIDTH), f32), w_pool_bf, scale, 0)
        carry0 = jnp.zeros((N_EXPERTS, 128), f32)
        x1p, hp, idx_p, gate_p, rank_p, carry1 = _out_proj(
            a, o_pool.reshape(tp, POOL_WIDTH), xp, wa, wp, g_ffn, wr_hi, wr_lo, br, tri[tm_p], carry0)
        kp_l.append(k.reshape(bp, N_HEADS, 2, QK_DIM, sp).transpose(0, 4, 1, 2, 3))
        vp_l.append(v.reshape(bp, sp, N_HEADS, V_DIM))
        pp_l.append(u3[:, sp - POOL_STATE:])

        q, kb, vb, k, v, u = _in_proj_sample(xs, g_attn, w_in_bf, qg.reshape(1, QK_WIDTH),
                                             kg.reshape(1, QK_WIDTH), seg)
        a = _sample_attention(q, kb, vb, cache_k[l].reshape(bs, past, QK_WIDTH).transpose(0, 2, 1),
                              cache_v[l].reshape(bs, past * N_HEADS, V_DIM), lam_rows, sub.reshape(1, V_DIM),
                              bs, ss, lam_init)
        u3 = u.reshape(bs, ss, POOL_WIDTH)
        hist = jnp.concatenate([jnp.zeros((bs, 1, POOL_WIDTH), f32), state_pool[l]], axis=1)
        o_pool = _pool(u3, hist, w_pool_bf, scale, past)
        x1s, hs, idx_s, gate_s, rank_s, carry2 = _out_proj(
            a, o_pool.reshape(ts, POOL_WIDTH), xs, wa, wp, g_ffn, wr_hi, wr_lo, br, tri[tm_s], carry1)
        ks_l.append(k.reshape(bs, ss, N_HEADS, 2, QK_DIM))
        vs_l.append(v.reshape(bs, ss, N_HEADS, V_DIM))
        ps_l.append(jnp.concatenate([state_pool[l], u3], axis=1)[:, -POOL_STATE:])

        xp, xs = _moe([(hp, idx_p, gate_p, rank_p, x1p), (hs, idx_s, gate_s, rank_s, x1s)],
                      carry2[:, 0], w_gate[l], bg, w_up[l], bu, w_down[l], bd)

    return (xp.reshape(bp, sp, d), xs.reshape(bs, ss, d), jnp.stack(kp_l), jnp.stack(vp_l),
            jnp.stack(pp_l), jnp.stack(ks_l), jnp.stack(vs_l), jnp.stack(ps_l))
```

```python
import functools
import math

import jax
import jax.numpy as jnp
from jax import lax
from jax.experimental import pallas as pl
from jax.experimental.pallas import tpu as pltpu

CHUNK = 64
N_HEADS = 4
QK_DIM = 64
V_DIM = 128
HEAD_W = 2 * QK_DIM
QK_WIDTH = N_HEADS * HEAD_W
ATTN_WIDTH = N_HEADS * V_DIM
POOL_WINDOWS = (2, 4, 8, 16)
POOL_GROUP = 128
POOL_WIDTH = POOL_GROUP * len(POOL_WINDOWS)
POOL_STATE = max(POOL_WINDOWS) - 1
HALO = POOL_STATE + 1
N_EXPERTS = 32
TOP_K = 4
SWIGLU_LIMIT = 7.0
SWIGLU_ALPHA = 1.702
EPS = 1e-6
LOG2E = 1.4426950408889634
D_MODEL = 1024
LANES = 128
ROW_SUB = D_MODEL // LANES

ROW_TILE = 512
ATT_BLOCK = 256
ATT_HEADS = 2
MOE_TILE = 512
DMA_UNROLL = 8
ZERO_ROWS = 64
MAX_FIXED_SHIFT = 60.0
VMEM_LIMIT = 56 * 1024 * 1024


def _pick_tile(n, cap, mult=8):
    t = min(n, cap)
    while n % t or t % mult:
        t -= 1
    return t


def _split_bf16(x):
    hi = x.astype(jnp.bfloat16)
    lo = (x - hi.astype(jnp.float32)).astype(jnp.bfloat16)
    return hi, lo


def _dot_nt(a, b):
    return lax.dot_general(a, b, (((1,), (1,)), ((), ())), preferred_element_type=jnp.float32)


def _dot(a, b):
    return jnp.dot(a, b, preferred_element_type=jnp.float32)


def _store_row_tiles(ref, x):
    m = x.shape[0]
    for c in range(ROW_SUB):
        ref[pl.ds(c, m, stride=ROW_SUB), :] = x[:, c * LANES:(c + 1) * LANES]


def _load_row_tiles(ref, m):
    return jnp.concatenate([ref[pl.ds(c, m, stride=ROW_SUB), :] for c in range(ROW_SUB)], axis=1)


def _normed_bf16(x_ref, g_ref):
    x = x_ref[...]
    ms = jnp.mean(x * x, axis=-1, keepdims=True)
    return (x * lax.rsqrt(ms + EPS) * g_ref[...]).astype(jnp.bfloat16)


def _in_proj_prompt_kernel(x_ref, g_ref, wvu_ref, wqkt_ref, gain_ref,
                           q_ref, kb_ref, vt_ref, k_ref, v_ref, u_ref, *, blk):
    hn = _normed_bf16(x_ref, g_ref)
    tm = hn.shape[0]
    z = _dot(hn, wvu_ref[...])
    v = z[:, :ATTN_WIDTH]
    for c in range(ATTN_WIDTH // LANES):
        v_ref[pl.ds(c, tm, stride=ATTN_WIDTH // LANES), :] = v[:, c * LANES:(c + 1) * LANES]
    u_ref[...] = z[:, ATTN_WIDTH:]

    t = _dot_nt(wqkt_ref[...], hn)
    t3 = t.reshape(2 * QK_WIDTH // QK_DIM, QK_DIM, tm)
    r = lax.rsqrt(jnp.mean(t3 * t3, axis=1, keepdims=True) + EPS)
    tn = (t3 * r).reshape(2 * QK_WIDTH, tm) * gain_ref[...]
    qn, kn = tn[:QK_WIDTH], tn[QK_WIDTH:]
    k_ref[...] = kn
    kb_ref[...] = kn.T.astype(jnp.bfloat16)
    vt = v.T
    for s in range(tm // blk):
        q_ref[s] = qn[:, s * blk:(s + 1) * blk].astype(jnp.bfloat16)
        vt_ref[s] = vt[:, s * blk:(s + 1) * blk].astype(jnp.bfloat16)


def _in_proj_prompt(x, g, wvu, wqkt, gain, batch, seq, blk):
    t, d = x.shape
    tm = _pick_tile(seq, ROW_TILE, blk)
    per_seq = seq // tm
    row = lambda w: pl.BlockSpec((tm, w), lambda i: (i, 0))
    full = lambda a: pl.BlockSpec(a.shape, lambda i: (0, 0))
    slab = pl.BlockSpec((tm // blk, QK_WIDTH, blk), lambda i: (i, 0, 0))
    f32, bf16 = jnp.float32, jnp.bfloat16
    kern = functools.partial(_in_proj_prompt_kernel, blk=blk)
    return pl.pallas_call(
        kern,
        grid=(t // tm,),
        in_specs=[row(d), full(g), full(wvu), full(wqkt), full(gain)],
        out_specs=[slab, row(QK_WIDTH), slab,
                   pl.BlockSpec((None, QK_WIDTH, tm), lambda i: (i // per_seq, 0, i % per_seq)),
                   pl.BlockSpec((tm * N_HEADS, V_DIM), lambda i: (i, 0)),
                   row(POOL_WIDTH)],
        out_shape=[jax.ShapeDtypeStruct((t // blk, QK_WIDTH, blk), bf16),
                   jax.ShapeDtypeStruct((t, QK_WIDTH), bf16),
                   jax.ShapeDtypeStruct((t // blk, ATTN_WIDTH, blk), bf16),
                   jax.ShapeDtypeStruct((batch, QK_WIDTH, seq), f32),
                   jax.ShapeDtypeStruct((t * N_HEADS, V_DIM), f32),
                   jax.ShapeDtypeStruct((t, POOL_WIDTH), f32)],
        compiler_params=pltpu.CompilerParams(dimension_semantics=("arbitrary",),
                                             vmem_limit_bytes=VMEM_LIMIT),
        name="in_proj_prompt",
    )(x, g, wvu, wqkt, gain)


def _in_proj_sample_kernel(x_ref, g_ref, w_ref, qg_ref, kg_ref, seg_ref,
                           q_ref, kb_ref, vb_ref, k_ref, v_ref, u_ref):
    hn = _normed_bf16(x_ref, g_ref)
    z = _dot(hn, w_ref[...])
    seg = seg_ref[...]

    def qk_norm(t, gain):
        hi, lo = _split_bf16(t * t)
        ss = _dot(hi, seg) + _dot(lo, seg)
        return t * lax.rsqrt(ss * (1.0 / QK_DIM) + EPS) * gain

    qn = qk_norm(z[:, :QK_WIDTH], qg_ref[...])
    kn = qk_norm(z[:, QK_WIDTH:2 * QK_WIDTH], kg_ref[...])
    v = z[:, 2 * QK_WIDTH:2 * QK_WIDTH + ATTN_WIDTH]
    q_ref[...] = (qn * (QK_DIM ** -0.5)).astype(jnp.bfloat16)
    k_ref[...] = kn
    kb_ref[...] = kn.astype(jnp.bfloat16)
    v_ref[...] = v
    vb_ref[...] = v.astype(jnp.bfloat16)
    u_ref[...] = z[:, 2 * QK_WIDTH + ATTN_WIDTH:]


def _in_proj_sample(x, g, w_bf, qg, kg, seg):
    t, d = x.shape
    tm = _pick_tile(t, ROW_TILE)
    row = lambda w: pl.BlockSpec((tm, w), lambda i: (i, 0))
    full = lambda a: pl.BlockSpec(a.shape, lambda i: (0, 0))
    f32, bf16 = jnp.float32, jnp.bfloat16
    return pl.pallas_call(
        _in_proj_sample_kernel,
        grid=(t // tm,),
        in_specs=[row(d), full(g), full(w_bf), full(qg), full(kg), full(seg)],
        out_specs=[row(QK_WIDTH)] * 2 + [row(ATTN_WIDTH), row(QK_WIDTH), row(ATTN_WIDTH), row(POOL_WIDTH)],
        out_shape=[jax.ShapeDtypeStruct((t, QK_WIDTH), bf16),
                   jax.ShapeDtypeStruct((t, QK_WIDTH), bf16),
                   jax.ShapeDtypeStruct((t, ATTN_WIDTH), bf16),
                   jax.ShapeDtypeStruct((t, QK_WIDTH), f32),
                   jax.ShapeDtypeStruct((t, ATTN_WIDTH), f32),
                   jax.ShapeDtypeStruct((t, POOL_WIDTH), f32)],
        compiler_params=pltpu.CompilerParams(dimension_semantics=("arbitrary",),
                                             vmem_limit_bytes=VMEM_LIMIT),
        name="in_proj_sample",
    )(x, g, w_bf, qg, kg, seg)


def _lambda_scalar(lam_ref, lam_init):
    l = lam_ref[...]
    a = jnp.sum(l[0:1] * l[1:2], axis=-1, keepdims=True)
    b = jnp.sum(l[2:3] * l[3:4], axis=-1, keepdims=True)
    return jnp.exp(a) - jnp.exp(b) + lam_init


def _split_maps(q, axis):
    pos = lax.broadcasted_iota(jnp.int32, q.shape, axis)
    zero = jnp.zeros_like(q)
    return jnp.where(pos < QK_DIM, q, zero), jnp.where(pos >= QK_DIM, q, zero)


def _prompt_attn_kernel(bound_ref, lam_ref, subln_ref, *refs, lam_init, blk, heads):
    q_refs, k_refs, v_refs = refs[:heads], refs[heads:2 * heads], refs[2 * heads:3 * heads]
    o_ref, s_even, s_odd, p_even, p_odd, acc_ref = refs[3 * heads:]
    nq = q_refs[0].shape[0]
    cur = {}
    bound = bound_ref[0, 0]
    lam = _lambda_scalar(lam_ref, lam_init)
    n_chain = 2 * heads

    def scores(j, s_buf):
        rows = pl.ds(pl.multiple_of(j * blk, blk), blk)
        for h in range(heads):
            kb = k_refs[h][rows, :]
            for c in range(2):
                s_buf[2 * h + c] = _dot(kb, cur["q"][2 * h + c])

    def stage(carry, s_cur, p_prev, v_prev, p_cur, masked, nxt, fixed):
        if nxt is not None:
            scores(*nxt)
        out = []
        for n in range(n_chain):
            h = n // 2
            m, l = carry[n]
            pv = _dot(v_refs[h][v_prev], p_prev[n])
            s = s_cur[n]
            if masked:
                kc = lax.broadcasted_iota(jnp.int32, s.shape, 0) // CHUNK
                qc = lax.broadcasted_iota(jnp.int32, s.shape, 1) // CHUNK
                s = jnp.where(kc <= qc, s, -jnp.inf)
            if fixed:
                p = jnp.exp2(s - bound)
                acc_ref[n] = acc_ref[n] + pv
                out.append((m, l + jnp.sum(p, axis=0, keepdims=True)))
            else:
                m_new = jnp.maximum(m, jnp.max(s, axis=0, keepdims=True))
                alpha = jnp.exp2(m - m_new)
                p = jnp.exp2(s - m_new)
                acc_ref[n] = (acc_ref[n] + pv) * alpha
                out.append((m_new, alpha * l + jnp.sum(p, axis=0, keepdims=True)))
            p_cur[n] = p.astype(jnp.bfloat16)
        return tuple(out)

    def finish(carry, p_last):
        for h in range(heads):
            vb = v_refs[h][cur["i"]]
            o = [(acc_ref[2 * h + c] + _dot(vb, p_last[2 * h + c])) / carry[2 * h + c][1] for c in range(2)]
            o = o[0] - lam * o[1]
            ms = jnp.mean(o * o, axis=0, keepdims=True)
            o = o * lax.rsqrt(ms + EPS) * (subln_ref[...] * (1.0 - lam_init))
            rows = pl.ds(pl.multiple_of(cur["i"] * blk, blk), blk)
            o_ref[rows, h * V_DIM:(h + 1) * V_DIM] = o.T.astype(o_ref.dtype)

    def run(fixed):
        def body(i, c):
            qblock(i, fixed)
            return c
        lax.fori_loop(0, nq, body, 0)

    def qblock(i, fixed):
        cur["i"] = i
        cur["q"] = []
        for h in range(heads):
            cur["q"].extend(_split_maps(q_refs[h][i], 0))
        p_odd[...] = jnp.zeros_like(p_odd)
        acc_ref[...] = jnp.zeros_like(acc_ref)
        scores(0, s_even)
        neg = jnp.full((1, blk), -jnp.inf, jnp.float32)
        zl = jnp.zeros((1, blk), jnp.float32)

        def pair(jj, carry):
            j = 2 * jj
            carry = stage(carry, s_even, p_odd, jnp.maximum(j - 1, 0), p_even, False, (j + 1, s_odd), fixed)
            return stage(carry, s_odd, p_even, j, p_odd, False, (j + 2, s_even), fixed)

        carry = lax.fori_loop(0, i // 2, pair, ((neg, zl),) * n_chain)

        @pl.when(i % 2 == 0)
        def _():
            finish(stage(carry, s_even, p_odd, jnp.maximum(i - 1, 0), p_even, True, None, fixed), p_even)

        @pl.when(i % 2 == 1)
        def _():
            c = stage(carry, s_even, p_odd, jnp.maximum(i - 2, 0), p_even, False, (i, s_odd), fixed)
            finish(stage(c, s_odd, p_even, i - 1, p_odd, True, None, fixed), p_odd)

    small = bound <= MAX_FIXED_SHIFT
    pl.when(small)(functools.partial(run, True))
    pl.when(jnp.logical_not(small))(functools.partial(run, False))


def _prompt_attention(bound, qt, k, vt, lam_rows, subln_col, batch, seq, blk, lam_init):
    nq = seq // blk
    t = batch * seq
    heads = ATT_HEADS
    kern = functools.partial(_prompt_attn_kernel, lam_init=lam_init, blk=blk, heads=heads)
    s_buf = pltpu.VMEM((2 * heads, blk, blk), jnp.float32)
    p_buf = pltpu.VMEM((2 * heads, blk, blk), jnp.bfloat16)
    return pl.pallas_call(
        kern,
        grid=(batch, N_HEADS // heads),
        in_specs=[pl.BlockSpec(memory_space=pltpu.SMEM),
                  pl.BlockSpec(lam_rows.shape, lambda b, h: (0, 0)),
                  pl.BlockSpec(subln_col.shape, lambda b, h: (0, 0))]
        + [pl.BlockSpec((nq, HEAD_W, blk), lambda b, h, e=e: (b, h * heads + e, 0)) for e in range(heads)]
        + [pl.BlockSpec((seq, HEAD_W), lambda b, h, e=e: (b, h * heads + e)) for e in range(heads)]
        + [pl.BlockSpec((nq, V_DIM, blk), lambda b, h, e=e: (b, h * heads + e, 0)) for e in range(heads)],
        out_specs=pl.BlockSpec((seq, heads * V_DIM), lambda b, h: (b, h)),
        out_shape=jax.ShapeDtypeStruct((t, ATTN_WIDTH), jnp.bfloat16),
        scratch_shapes=[s_buf, s_buf, p_buf, p_buf, pltpu.VMEM((2 * heads, V_DIM, blk), jnp.float32)],
        compiler_params=pltpu.CompilerParams(dimension_semantics=("arbitrary",) * 2,
                                             vmem_limit_bytes=VMEM_LIMIT),
        name="prompt_attention",
    )(bound, lam_rows, subln_col, *([qt] * heads + [k] * heads + [vt] * heads))


def _sample_attn_kernel(lam_ref, subln_ref, q_ref, kn_ref, vn_ref, ck_ref, cv_ref, o_ref, *, lam_init, past):
    lam = _lambda_scalar(lam_ref, lam_init)
    q = q_ref[...]
    nq = q.shape[0]
    qpos = past + lax.broadcasted_iota(jnp.int32, (nq, nq), 0)
    kpos = past + lax.broadcasted_iota(jnp.int32, (nq, nq), 1)
    vis_new = (kpos // CHUNK) <= (qpos // CHUNK)
    for h in range(N_HEADS):
        qk = slice(h * HEAD_W, (h + 1) * HEAD_W)
        vv = slice(h * V_DIM, (h + 1) * V_DIM)
        q1, q2 = _split_maps(q[:, qk], 1)
        kc = ck_ref[0, qk, :].astype(jnp.bfloat16)
        vc = cv_ref[0, pl.ds(h, past, stride=N_HEADS), :].astype(jnp.bfloat16)
        kn = kn_ref[:, qk]
        vn = vn_ref[:, vv]

        def one_map(qm):
            sc = _dot(qm, kc)
            sn = jnp.where(vis_new, _dot_nt(qm, kn), -jnp.inf)
            m = jnp.maximum(jnp.max(sc, axis=-1, keepdims=True), jnp.max(sn, axis=-1, keepdims=True))
            pc = jnp.exp(sc - m)
            pn = jnp.exp(sn - m)
            l = jnp.sum(pc, axis=-1, keepdims=True) + jnp.sum(pn, axis=-1, keepdims=True)
            return (_dot(pc.astype(jnp.bfloat16), vc) + _dot(pn.astype(jnp.bfloat16), vn)) / l

        o = one_map(q1) - lam * one_map(q2)
        ms = jnp.mean(o * o, axis=-1, keepdims=True)
        o_ref[:, vv] = (o * lax.rsqrt(ms + EPS) * subln_ref[...] * (1.0 - lam_init)).astype(o_ref.dtype)


def _sample_attention(q, k_new, v_new, cache_k, cache_v, lam_rows, subln, batch, nq, lam_init):
    past = cache_k.shape[2]
    kern = functools.partial(_sample_attn_kernel, lam_init=lam_init, past=past)
    row = lambda w: pl.BlockSpec((nq, w), lambda b: (b, 0))
    return pl.pallas_call(
        kern,
        grid=(batch,),
        in_specs=[pl.BlockSpec(lam_rows.shape, lambda b: (0, 0)),
                  pl.BlockSpec(subln.shape, lambda b: (0, 0)),
                  row(QK_WIDTH), row(QK_WIDTH), row(ATTN_WIDTH),
                  pl.BlockSpec((1, QK_WIDTH, past), lambda b: (b, 0, 0)),
                  pl.BlockSpec((1, past * N_HEADS, V_DIM), lambda b: (b, 0, 0))],
        out_specs=row(ATTN_WIDTH),
        out_shape=jax.ShapeDtypeStruct((batch * nq, ATTN_WIDTH), jnp.bfloat16),
        compiler_params=pltpu.CompilerParams(dimension_semantics=("arbitrary",),
                                             vmem_limit_bytes=VMEM_LIMIT),
        name="sample_attention",
    )(lam_rows, subln, q, k_new, v_new, cache_k, cache_v)


def _pool_kernel(u_ref, prev_ref, hist_ref, w_ref, scale_ref, o_ref, ext_ref, *, pos0, tq):
    j = pl.program_id(1)
    ext_ref[0:HALO, :] = jnp.where(j == 0, hist_ref[0], prev_ref[0])
    ext_ref[HALO:, :] = u_ref[0]
    pos = pos0 + j * tq + lax.broadcasted_iota(jnp.int32, (tq, 1), 0)
    for g, w in enumerate(POOL_WINDOWS):
        cols = slice(g * POOL_GROUP, (g + 1) * POOL_GROUP)
        acc = ext_ref[HALO:, cols]
        for back in range(1, w):
            acc = acc + ext_ref[HALO - back:HALO - back + tq, cols]
        count = jnp.minimum(pos + 1, w).astype(jnp.float32)
        d = acc / count - ext_ref[HALO:, cols]
        y = _dot(d.astype(jnp.bfloat16), w_ref[g])
        o_ref[0, :, cols] = (y * scale_ref[:, cols]).astype(o_ref.dtype)


def _pool(u, hist, w_pool_bf, scale, pos0):
    b, q, c = u.shape
    tq = _pick_tile(q, ROW_TILE, HALO)
    per = tq // HALO
    kern = functools.partial(_pool_kernel, pos0=pos0, tq=tq)
    return pl.pallas_call(
        kern,
        grid=(b, q // tq),
        in_specs=[pl.BlockSpec((1, tq, c), lambda i, j: (i, j, 0)),
                  pl.BlockSpec((1, HALO, c), lambda i, j: (i, jnp.maximum(j * per - 1, 0), 0)),
                  pl.BlockSpec((1, HALO, c), lambda i, j: (i, 0, 0)),
                  pl.BlockSpec(w_pool_bf.shape, lambda i, j: (0, 0, 0)),
                  pl.BlockSpec(scale.shape, lambda i, j: (0, 0))],
        out_specs=pl.BlockSpec((1, tq, c), lambda i, j: (i, j, 0)),
        out_shape=jax.ShapeDtypeStruct((b, q, c), jnp.bfloat16),
        scratch_shapes=[pltpu.VMEM((HALO + tq, c), jnp.float32)],
        compiler_params=pltpu.CompilerParams(dimension_semantics=("arbitrary",) * 2,
                                             vmem_limit_bytes=VMEM_LIMIT),
        name="pool",
    )(u, u, hist, w_pool_bf, scale)


def _out_proj_kernel(a_ref, p_ref, x_ref, wa_ref, wp_ref, g_ref, wr_hi_ref, wr_lo_ref, br_ref,
                     tri_ref, cin_ref,
                     x1_ref, h_ref, idx_ref, gate_ref, rank_ref, cout_ref, carry_ref):
    i = pl.program_id(0)

    @pl.when(i == 0)
    def _():
        carry_ref[...] = cin_ref[...]

    x1 = x_ref[...] + _dot(a_ref[...], wa_ref[...]) + _dot(p_ref[...], wp_ref[...])
    x1_ref[...] = x1
    ms = jnp.mean(x1 * x1, axis=-1, keepdims=True)
    h = x1 * lax.rsqrt(ms + EPS) * g_ref[...]
    _store_row_tiles(h_ref, h)

    h_hi, h_lo = _split_bf16(h)
    wr_hi = wr_hi_ref[...]
    logits = _dot_nt(wr_hi, h_hi) + _dot_nt(wr_hi, h_lo) + _dot_nt(wr_lo_ref[...], h_hi) + br_ref[...]
    expert = lax.broadcasted_iota(jnp.int32, logits.shape, 0)
    work = logits
    vals, idxs, hots = [], [], []
    for _ in range(TOP_K):
        m = jnp.max(work, axis=0, keepdims=True)
        sel = jnp.min(jnp.where(work == m, expert, N_EXPERTS), axis=0, keepdims=True)
        hot = expert == sel
        work = jnp.where(hot, -jnp.inf, work)
        vals.append(m)
        idxs.append(sel)
        hots.append(hot)
    ex = [jnp.exp(v - vals[0]) for v in vals]
    den = ex[0] + ex[1] + ex[2] + ex[3]
    gate_ref[...] = jnp.concatenate([e / den for e in ex] + [jnp.zeros_like(den)] * (8 - TOP_K), axis=0)
    idx_ref[...] = jnp.concatenate(idxs, axis=1)

    hot_all = hots[0] | hots[1] | hots[2] | hots[3]
    hot_f = jnp.where(hot_all, 1.0, 0.0)
    prefix = _dot(hot_f.astype(jnp.bfloat16), tri_ref[...]) + carry_ref[:, 0:1]
    ranks = [jnp.sum(jnp.where(hot, prefix, 0.0), axis=0, keepdims=True) for hot in hots]
    rank_ref[...] = jnp.concatenate(ranks, axis=1).astype(jnp.int32)
    carry_ref[...] = carry_ref[...] + jnp.sum(hot_f, axis=1, keepdims=True)
    cout_ref[...] = carry_ref[...]


def _out_proj(a, p, x, wa, wp, g, wr_hi, wr_lo, br, tri, carry_in):
    t, d = x.shape
    tm = tri.shape[0]
    row = lambda w: pl.BlockSpec((tm, w), lambda i: (i, 0))
    lst = pl.BlockSpec((None, 1, TOP_K * tm), lambda i: (i, 0, 0))
    full = lambda arr: pl.BlockSpec(arr.shape, lambda i: (0, 0))
    f32, i32 = jnp.float32, jnp.int32
    n_tiles = t // tm
    return pl.pallas_call(
        _out_proj_kernel,
        grid=(n_tiles,),
        in_specs=[row(ATTN_WIDTH), row(POOL_WIDTH), row(d), full(wa), full(wp), full(g),
                  full(wr_hi), full(wr_lo), full(br), full(tri), full(carry_in)],
        out_specs=[row(d), pl.BlockSpec((tm * ROW_SUB, LANES), lambda i: (i, 0)), lst,
                   pl.BlockSpec((8, tm), lambda i: (0, i)), lst, full(carry_in)],
        out_shape=[jax.ShapeDtypeStruct((t, d), f32),
                   jax.ShapeDtypeStruct((t * ROW_SUB, LANES), f32),
                   jax.ShapeDtypeStruct((n_tiles, 1, TOP_K * tm), i32),
                   jax.ShapeDtypeStruct((8, t), f32),
                   jax.ShapeDtypeStruct((n_tiles, 1, TOP_K * tm), i32),
                   jax.ShapeDtypeStruct(carry_in.shape, f32)],
        scratch_shapes=[pltpu.VMEM(carry_in.shape, f32)],
        compiler_params=pltpu.CompilerParams(dimension_semantics=("arbitrary",),
                                             vmem_limit_bytes=VMEM_LIMIT),
        name="out_proj_router",
    )(a, p, x, wa, wp, g, wr_hi, wr_lo, br, tri, carry_in)


def _fetch_slots(slots_hbm, slots_smem, sem, tile):
    return pltpu.make_async_copy(slots_hbm.at[tile, 0], slots_smem, sem)


def _row_tile(ref, row):
    return ref.at[pl.ds(pl.multiple_of(row * ROW_SUB, ROW_SUB), ROW_SUB)]


def _row_tile_span(ref, row, rows):
    return ref.at[pl.ds(pl.multiple_of(row * ROW_SUB, ROW_SUB), rows * ROW_SUB)]


def _dispatch_kernel(pend_ref, padded_ref, slots_hbm, *rest, tm, group_tiles, n_blocks):
    n_groups = len(group_tiles)
    h_refs = rest[:n_groups]
    xs_out, slots_smem, zeros, slot_sem, row_sem = rest[n_groups:]
    i = pl.program_id(0)
    n = pl.num_programs(0)
    zr = zeros.shape[0] // ROW_SUB
    per_block = MOE_TILE // zr

    def zero_block(first_row):
        for c in range(per_block):
            pltpu.make_async_copy(zeros, _row_tile_span(xs_out, first_row + c * zr, zr), row_sem).start()

    def wait_zero_blocks(count):
        def body(_, c):
            pltpu.make_async_copy(zeros, _row_tile_span(xs_out, 0, zr), row_sem).wait()
            return c
        lax.fori_loop(0, count * per_block, body, 0)

    @pl.when(i == 0)
    def _():
        _fetch_slots(slots_hbm, slots_smem, slot_sem, 0).start()
        zeros[...] = jnp.zeros_like(zeros)
        for e in range(N_EXPERTS):
            @pl.when(padded_ref[e] > 0)
            def _():
                zero_block(pend_ref[e] - MOE_TILE)
                wait_zero_blocks(1)
        used = pend_ref[N_EXPERTS - 1] // MOE_TILE

        def spare(b, c):
            zero_block(b * MOE_TILE)
            return c
        lax.fori_loop(used, n_blocks, spare, 0)
        wait_zero_blocks(n_blocks - used)

    _fetch_slots(slots_hbm, slots_smem, slot_sem, i).wait()

    def scatter(h_ref):
        def issue(g, c):
            for rr in range(DMA_UNROLL):
                r = g * DMA_UNROLL + rr
                for k in range(TOP_K):
                    dst = slots_smem[k * tm + r]
                    pltpu.make_async_copy(_row_tile(h_ref, r), _row_tile(xs_out, dst), row_sem).start(priority=k % 2)
            return c

        lax.fori_loop(0, tm // DMA_UNROLL, issue, 0)

        @pl.when(i + 1 < n)
        def _():
            _fetch_slots(slots_hbm, slots_smem, slot_sem, i + 1).start()

        for _ in range(TOP_K):
            pltpu.make_async_copy(h_ref, _row_tile_span(xs_out, 0, tm), row_sem).wait()

    first = 0
    for h_ref, tiles in zip(h_refs, group_tiles):
        pl.when((i >= first) & (i < first + tiles))(functools.partial(scatter, h_ref))
        first += tiles


def _dispatch(pad_end, padded, slots, h_groups, n_pad):
    n_tiles, _, per = slots.shape
    tm = per // TOP_K
    group_tiles = tuple(h.shape[0] // (tm * ROW_SUB) for h in h_groups)
    assert sum(group_tiles) == n_tiles
    kern = functools.partial(_dispatch_kernel, tm=tm, group_tiles=group_tiles, n_blocks=n_pad // MOE_TILE)
    any_spec = pl.BlockSpec(memory_space=pl.ANY)
    in_specs, first = [any_spec], 0
    for tiles in group_tiles:
        in_specs.append(pl.BlockSpec(
            (tm * ROW_SUB, LANES),
            lambda i, pe, pd, first=first, tiles=tiles: (jnp.clip(i - first, 0, tiles - 1), 0)))
        first += tiles
    return pl.pallas_call(
        kern,
        grid_spec=pltpu.PrefetchScalarGridSpec(
            num_scalar_prefetch=2,
            grid=(n_tiles,),
            in_specs=in_specs,
            out_specs=any_spec,
            scratch_shapes=[pltpu.SMEM((per,), jnp.int32),
                            pltpu.VMEM((ZERO_ROWS * ROW_SUB, LANES), jnp.float32),
                            pltpu.SemaphoreType.DMA(()), pltpu.SemaphoreType.DMA(())]),
        out_shape=jax.ShapeDtypeStruct((n_pad * ROW_SUB, LANES), jnp.float32),
        compiler_params=pltpu.CompilerParams(dimension_semantics=("arbitrary",),
                                             vmem_limit_bytes=VMEM_LIMIT),
        name="dispatch",
    )(pad_end, padded, slots, *h_groups)


def _combine_kernel(slots_hbm, os_hbm, x1_ref, gate_ref, y_ref, slots_smem, buf_a, buf_b,
                    slot_sem, sem_a, sem_b, *, tm):
    i = pl.program_id(0)
    n = pl.num_programs(0)

    def gather(buf, sem):
        _fetch_slots(slots_hbm, slots_smem, slot_sem, 0).wait()

        def issue(g, c):
            for rr in range(DMA_UNROLL):
                r = g * DMA_UNROLL + rr
                for k in range(TOP_K):
                    src = slots_smem[k * tm + r]
                    pltpu.make_async_copy(_row_tile(os_hbm, src), _row_tile(buf.at[k], r), sem).start(priority=k % 2)
            return c

        lax.fori_loop(0, tm // DMA_UNROLL, issue, 0)

    def gather_next(buf, sem):
        @pl.when(i + 1 < n)
        def _():
            gather(buf, sem)

            @pl.when(i + 2 < n)
            def _():
                _fetch_slots(slots_hbm, slots_smem, slot_sem, i + 2).start()

    def reduce(buf, sem):
        for k in range(TOP_K):
            pltpu.make_async_copy(_row_tile_span(os_hbm, 0, tm), buf.at[k], sem).wait()
        gate = gate_ref[...].T
        for c in range(ROW_SUB):
            cols = slice(c * LANES, (c + 1) * LANES)
            y = x1_ref[:, cols]
            for k in range(TOP_K):
                y = y + gate[:, k:k + 1] * buf[k, pl.ds(c, tm, stride=ROW_SUB), :]
            y_ref[:, cols] = y

    @pl.when(i == 0)
    def _():
        _fetch_slots(slots_hbm, slots_smem, slot_sem, 0).start()
        gather(buf_a, sem_a)

        @pl.when(n > 1)
        def _():
            _fetch_slots(slots_hbm, slots_smem, slot_sem, 1).start()

    @pl.when(i % 2 == 0)
    def _():
        gather_next(buf_b, sem_b)
        reduce(buf_a, sem_a)

    @pl.when(i % 2 == 1)
    def _():
        gather_next(buf_a, sem_a)
        reduce(buf_b, sem_b)


def _combine(slots, out_sorted, x1, gate_rows):
    n_tiles, _, per = slots.shape
    tm = per // TOP_K
    t, d = x1.shape
    kern = functools.partial(_combine_kernel, tm=tm)
    buf = pltpu.VMEM((TOP_K, tm * ROW_SUB, LANES), jnp.float32)
    dma = pltpu.SemaphoreType.DMA(())
    return pl.pallas_call(
        kern,
        grid=(n_tiles,),
        in_specs=[pl.BlockSpec(memory_space=pl.ANY),
                  pl.BlockSpec(memory_space=pl.ANY),
                  pl.BlockSpec((tm, d), lambda i: (i, 0)),
                  pl.BlockSpec((8, tm), lambda i: (0, i))],
        out_specs=pl.BlockSpec((tm, d), lambda i: (i, 0)),
        out_shape=jax.ShapeDtypeStruct((t, d), jnp.float32),
        scratch_shapes=[pltpu.SMEM((per,), jnp.int32), buf, buf, dma, dma, dma],
        compiler_params=pltpu.CompilerParams(dimension_semantics=("arbitrary",),
                                             vmem_limit_bytes=VMEM_LIMIT),
        name="combine",
    )(slots, out_sorted, x1, gate_rows)


def _expert_kernel(be_ref, nu_ref, x_ref, wg_ref, bg_ref, wu_ref, bu_ref, wd_ref, bd_ref, o_ref,
                   wg_bf, wu_bf, wd_bf):
    i = pl.program_id(0)
    tm = x_ref.shape[0] // ROW_SUB

    @pl.when(i < nu_ref[0])
    def _():
        @pl.when((i == 0) | (be_ref[i] != be_ref[jnp.maximum(i - 1, 0)]))
        def _():
            wg_bf[...] = wg_ref[0].astype(jnp.bfloat16)
            wu_bf[...] = wu_ref[0].astype(jnp.bfloat16)
            wd_bf[...] = wd_ref[0].astype(jnp.bfloat16)

        x = _load_row_tiles(x_ref, tm).astype(jnp.bfloat16)
        g = jnp.minimum(_dot(x, wg_bf[...]) + bg_ref[0], SWIGLU_LIMIT)
        u = jnp.clip(_dot(x, wu_bf[...]) + bu_ref[0], -SWIGLU_LIMIT, SWIGLU_LIMIT)
        a = (u + 1.0) * g * jax.nn.sigmoid(SWIGLU_ALPHA * g)
        _store_row_tiles(o_ref, _dot(a.astype(jnp.bfloat16), wd_bf[...]) + bd_ref[0])

    @pl.when(i >= nu_ref[0])
    def _():
        o_ref[...] = jnp.zeros_like(o_ref)


def _experts(block_e, n_used, x_sorted, wg, bg, wu, bu, wd, bd):
    d, f = wg.shape[1], wg.shape[2]
    blk_rows = MOE_TILE * ROW_SUB
    rows = lambda i, be, nu: (jnp.minimum(i, nu[0] - 1), 0)
    wsel = lambda i, be, nu: (be[i], 0, 0)
    return pl.pallas_call(
        _expert_kernel,
        grid_spec=pltpu.PrefetchScalarGridSpec(
            num_scalar_prefetch=2,
            grid=(x_sorted.shape[0] // blk_rows,),
            in_specs=[pl.BlockSpec((blk_rows, LANES), rows),
                      pl.BlockSpec((1, d, f), wsel), pl.BlockSpec((1, 1, f), wsel),
                      pl.BlockSpec((1, d, f), wsel), pl.BlockSpec((1, 1, f), wsel),
                      pl.BlockSpec((1, f, d), wsel), pl.BlockSpec((1, 1, d), wsel)],
            out_specs=pl.BlockSpec((blk_rows, LANES), lambda i, be, nu: (i, 0)),
            scratch_shapes=[pltpu.VMEM((d, f), jnp.bfloat16),
                            pltpu.VMEM((d, f), jnp.bfloat16),
                            pltpu.VMEM((f, d), jnp.bfloat16)]),
        out_shape=jax.ShapeDtypeStruct(x_sorted.shape, jnp.float32),
        compiler_params=pltpu.CompilerParams(dimension_semantics=("arbitrary",),
                                             vmem_limit_bytes=VMEM_LIMIT),
        name="experts",
    )(block_e, n_used, x_sorted, wg, bg, wu, bu, wd, bd)


def _moe(groups, counts, wg, bg, wu, bu, wd, bd):
    tm = MOE_TILE
    total = sum(g[4].shape[0] for g in groups)
    cnt = counts.astype(jnp.int32)
    padded = (cnt + tm - 1) // tm * tm
    pad_end = jnp.cumsum(padded)
    pad_start = pad_end - padded
    n_blocks = -(-(total * TOP_K) // tm) + N_EXPERTS
    n_pad = n_blocks * tm
    block_start = jnp.arange(n_blocks, dtype=jnp.int32) * tm
    block_e = jnp.minimum(jnp.sum(block_start[:, None] >= pad_end[None, :], axis=1),
                          N_EXPERTS - 1).astype(jnp.int32)
    n_used = (pad_end[-1:] // tm).astype(jnp.int32)
    experts = jnp.arange(N_EXPERTS, dtype=jnp.int32)

    slots = [rank + jnp.sum(jnp.where(idx[..., None] == experts, pad_start, 0), axis=-1)
             for _, idx, _, rank, _ in groups]
    x_sorted = _dispatch(pad_end, padded, jnp.concatenate(slots, axis=0), [g[0] for g in groups], n_pad)
    out_sorted = _experts(block_e, n_used, x_sorted, wg, bg, wu, bu, wd, bd)
    return [_combine(s, out_sorted, x1, gate) for s, (_, _, gate, _, x1) in zip(slots, groups)]


def kernel(x_prompt, x_sample, cache_k, cache_v, state_pool, attn_norm, w_in, q_norm, k_norm,
           lambda_q1, lambda_k1, lambda_q2, lambda_k2, subln, w_pool, pool_scale, w_out,
           ffn_norm, w_router, b_router, w_gate, b_gate, w_up, b_up, w_down, b_down):
    f32, bf16 = jnp.float32, jnp.bfloat16
    bp, sp, d = x_prompt.shape
    bs, ss, _ = x_sample.shape
    assert d == D_MODEL, "row-tiled expert buffers are laid out for D_MODEL-wide token rows"
    depth = w_in.shape[0]
    past = cache_k.shape[2]
    tp, ts = bp * sp, bs * ss
    xp = x_prompt.reshape(tp, d)
    xs = x_sample.reshape(ts, d)

    blk = _pick_tile(sp, ATT_BLOCK, CHUNK)
    tm_p = _pick_tile(tp, ROW_TILE, blk)
    tm_s = _pick_tile(ts, ROW_TILE)
    group = lax.broadcasted_iota(jnp.int32, (QK_WIDTH, QK_WIDTH), 0) // QK_DIM
    seg = (group == group.T).astype(bf16)
    tri = {tm: (lax.broadcasted_iota(jnp.int32, (tm, tm), 0)
                < lax.broadcasted_iota(jnp.int32, (tm, tm), 1)).astype(bf16) for tm in {tm_p, tm_s}}

    kp_l, vp_l, pp_l, ks_l, vs_l, ps_l = [], [], [], [], [], []
    for l in range(depth):
        lam_init = 0.8 - 0.6 * math.exp(-0.3 * l)
        lam_rows = jnp.stack([lambda_q1[l], lambda_k1[l], lambda_q2[l], lambda_k2[l]]).astype(f32)
        g_attn = attn_norm[l].reshape(1, d)
        w_in_bf = w_in[l].astype(bf16)
        wqkt = w_in_bf[:, :2 * QK_WIDTH].T
        wvu = w_in_bf[:, 2 * QK_WIDTH:]
        qg = jnp.tile(q_norm[l], QK_WIDTH // QK_DIM).astype(f32)
        kg = jnp.tile(k_norm[l], QK_WIDTH // QK_DIM).astype(f32)
        gain_col = jnp.concatenate([qg * (QK_DIM ** -0.5 * LOG2E), kg]).reshape(2 * QK_WIDTH, 1)
        bound = (QK_DIM ** 0.5 * LOG2E * jnp.max(jnp.abs(q_norm[l])) * jnp.max(jnp.abs(k_norm[l]))
                 ).astype(f32).reshape(1, 1)
        sub = subln[l].astype(f32)
        w_pool_bf = w_pool[l].astype(bf16)
        scale = pool_scale[l].reshape(1, POOL_WIDTH)
        wa = w_out[l][:ATTN_WIDTH].astype(bf16)
        wp = w_out[l][ATTN_WIDTH:].astype(bf16)
        g_ffn = ffn_norm[l].reshape(1, d)
        wr_hi, wr_lo = _split_bf16(w_router[l].T.astype(f32))
        br = b_router[l].reshape(N_EXPERTS, 1).astype(f32)
        bg = b_gate[l].reshape(N_EXPERTS, 1, -1)
        bu = b_up[l].reshape(N_EXPERTS, 1, -1)
        bd = b_down[l].reshape(N_EXPERTS, 1, -1)

        qt, kb, vt, k, v, u = _in_proj_prompt(xp, g_attn, wvu, wqkt, gain_col, bp, sp, blk)
        a = _prompt_attention(bound, qt, kb, vt, lam_rows, sub.reshape(V_DIM, 1), bp, sp, blk, lam_init)
        u3 = u.reshape(bp, sp, POOL_WIDTH)
        o_pool = _pool(u3, jnp.zeros((bp, HALO, POOL_WIDTH), f32), w_pool_bf, scale, 0)
        carry0 = jnp.zeros((N_EXPERTS, 128), f32)
        x1p, hp, idx_p, gate_p, rank_p, carry1 = _out_proj(
            a, o_pool.reshape(tp, POOL_WIDTH), xp, wa, wp, g_ffn, wr_hi, wr_lo, br, tri[tm_p], carry0)
        kp_l.append(k.reshape(bp, N_HEADS, 2, QK_DIM, sp).transpose(0, 4, 1, 2, 3))
        vp_l.append(v.reshape(bp, sp, N_HEADS, V_DIM))
        pp_l.append(u3[:, sp - POOL_STATE:])

        q, kb, vb, k, v, u = _in_proj_sample(xs, g_attn, w_in_bf, qg.reshape(1, QK_WIDTH),
                                             kg.reshape(1, QK_WIDTH), seg)
        a = _sample_attention(q, kb, vb, cache_k[l].reshape(bs, past, QK_WIDTH).transpose(0, 2, 1),
                              cache_v[l].reshape(bs, past * N_HEADS, V_DIM), lam_rows, sub.reshape(1, V_DIM),
                              bs, ss, lam_init)
        u3 = u.reshape(bs, ss, POOL_WIDTH)
        hist = jnp.concatenate([jnp.zeros((bs, 1, POOL_WIDTH), f32), state_pool[l]], axis=1)
        o_pool = _pool(u3, hist, w_pool_bf, scale, past)
        x1s, hs, idx_s, gate_s, rank_s, carry2 = _out_proj(
            a, o_pool.reshape(ts, POOL_WIDTH), xs, wa, wp, g_ffn, wr_hi, wr_lo, br, tri[tm_s], carry1)
        ks_l.append(k.reshape(bs, ss, N_HEADS, 2, QK_DIM))
        vs_l.append(v.reshape(bs, ss, N_HEADS, V_DIM))
        ps_l.append(jnp.concatenate([state_pool[l], u3], axis=1)[:, -POOL_STATE:])

        xp, xs = _moe([(hp, idx_p, gate_p, rank_p, x1p), (hs, idx_s, gate_s, rank_s, x1s)],
                      carry2[:, 0], w_gate[l], bg, w_up[l], bu, w_down[l], bd)

    return (xp.reshape(bp, sp, d), xs.reshape(bs, ss, d), jnp.stack(kp_l), jnp.stack(vp_l),
            jnp.stack(pp_l), jnp.stack(ks_l), jnp.stack(vs_l), jnp.stack(ps_l))
```
